```python
import math
import jax, jax.numpy as jnp
from jax import lax
import numpy as np

D_MODEL = 2048
BATCH = 8
SEQ = 2048
DEPTH = 1

CHUNK = 64
N_META = 16
EPS = 1e-5
ALPHA = (2 * DEPTH) ** 0.25
BETA = (8 * DEPTH) ** -0.25
MIX_WIDTH = D_MODEL
MIX_A = MIX_WIDTH // 2
MIX_B = MIX_WIDTH - MIX_A
A_HEAD_DIM = 128
A_HEADS = MIX_A // A_HEAD_DIM
A_KV_RANK = 256
IDX_HEADS = 16
IDX_DIM = 64
TOPK_MAX = 256
Q_BLOCK = 128
REL_BUCKETS = 32
REL_MAX_DIST = 128
G_HEADS = 4
G_VAL_DIM = MIX_B // G_HEADS
G_KEY_DIM = G_VAL_DIM // 2
G_GATE_RANK = 16
G_GATE_NORM = 16.0
P_HEADS = 8
P_NKEYS = 128
P_NEXPERTS = P_NKEYS * P_NKEYS
P_QDIM = 256
P_TOPK = 16
P_BLOCK = 256
SPLITS = (A_HEADS * A_HEAD_DIM,
          A_KV_RANK,
          IDX_HEADS * IDX_DIM,
          IDX_DIM,
          IDX_HEADS,
          G_HEADS * G_KEY_DIM,
          G_HEADS * G_KEY_DIM,
          G_HEADS * G_VAL_DIM,
          G_GATE_RANK,
          G_HEADS * G_VAL_DIM)
IN_COLS = sum(SPLITS)

kernel_name = 'hybrid_dsa_gla_peer_stream_encoder'


def layer_norm(x, g, b):
    xf = x.astype(jnp.float32)
    mu = jnp.mean(xf, axis=-1, keepdims=True)
    var = jnp.mean(jnp.square(xf - mu), axis=-1, keepdims=True)
    y = (xf - mu) * lax.rsqrt(var + EPS)
    return (y * g.astype(jnp.float32) + b.astype(jnp.float32)).astype(x.dtype)


def rms_norm_f32(x, g):
    xf = x.astype(jnp.float32)
    return xf * lax.rsqrt(jnp.mean(xf * xf, axis=-1, keepdims=True) + EPS) * g.astype(jnp.float32)


def chunk_id(pos):
    return jnp.where(pos < N_META, 0, 1 + (pos - N_META) // CHUNK)


def t5_bucket(rel):
    nb = REL_BUCKETS // 2
    max_exact = nb // 2
    base = jnp.where(rel > 0, nb, 0)
    n = jnp.abs(rel)
    nf = jnp.maximum(n, 1).astype(jnp.float32)
    large = max_exact + (jnp.log(nf / max_exact) / math.log(REL_MAX_DIST / max_exact)
                         * (nb - max_exact)).astype(jnp.int32)
    large = jnp.minimum(large, nb - 1)
    return base + jnp.where(n < max_exact, n, large)


def dsa_mixer(q, ckv, iq, ik, iw, w_uk, w_uv, rel_bias, topk):
    B, L = q.shape[0], q.shape[1]
    ql = jnp.einsum('blhd,hrd->blhr', q, w_uk) * (A_HEAD_DIM ** -0.5)
    iq = iq * (IDX_DIM ** -0.5)
    iw = iw * (IDX_HEADS ** -0.5)
    nblk = -(-L // Q_BLOCK)
    Lp = nblk * Q_BLOCK

    def to_blocks(a):
        a = jnp.pad(a, [(0, 0), (0, Lp - L)] + [(0, 0)] * (a.ndim - 2))
        return jnp.moveaxis(a.reshape((B, nblk, Q_BLOCK) + a.shape[2:]), 1, 0)

    kpos = jnp.arange(L, dtype=jnp.int32)
    kch = chunk_id(kpos)
    qpos = jnp.arange(Lp, dtype=jnp.int32).reshape(nblk, Q_BLOCK)

    def one_block(args):
        iq_b, iw_b, ql_b, qp = args
        adm = kch[None, :] <= chunk_id(qp)[:, None]
        logit = jnp.einsum('bqhd,bkd->bqhk', iq_b, ik)
        score = jnp.einsum('bqhk,bqh->bqk', jax.nn.relu(logit), iw_b).astype(jnp.float32)
        score = jnp.where(adm[None], score, -jnp.inf)
        _, sel = lax.top_k(score, topk)
        valid = jnp.take_along_axis(jnp.broadcast_to(adm[None], score.shape), sel, axis=-1)
        kv = jax.vmap(lambda c, s: c[s])(ckv, sel)
        bias = rel_bias[t5_bucket(kpos[sel] - qp[None, :, None])]
        logits = (jnp.einsum('bqhr,bqkr->bqhk', ql_b, kv).astype(jnp.float32)
                  + jnp.swapaxes(bias, -1, -2).astype(jnp.float32))
        logits = jnp.where(valid[:, :, None, :], logits, -jnp.inf)
        p = jax.nn.softmax(logits, axis=-1).astype(kv.dtype)
        return jnp.einsum('bqhk,bqkr->bqhr', p, kv)

    ol = lax.map(one_block, (to_blocks(iq), to_blocks(iw), to_blocks(ql), qpos))
    ol = jnp.moveaxis(ol, 0, 1).reshape(B, Lp, A_HEADS, A_KV_RANK)[:, :L]
    o = jnp.einsum('blhr,hrd->blhd', ol, w_uv)
    return o.reshape(B, L, A_HEADS * A_HEAD_DIM)


def gla_mixer(q, k, v, gk, og, norm_g):
    B, L = q.shape[0], q.shape[1]
    lead = CHUNK - N_META
    nC = (L + lead) // CHUNK

    def to_chunks(a):
        a = jnp.pad(a.astype(jnp.float32), [(0, 0), (lead, 0)] + [(0, 0)] * (a.ndim - 2))
        return jnp.moveaxis(a.reshape((B, nC, CHUNK) + a.shape[2:]), 1, 0)

    causal = jnp.tril(jnp.ones((CHUNK, CHUNK), dtype=bool))

    def step(S, inp):
        qc, kc, vc, gc = inp
        b = jnp.cumsum(gc, axis=1)
        o_inter = jnp.einsum('bchd,bhdv->bchv', qc * jnp.exp(b), S)
        diff = b[:, :, None] - b[:, None, :]
        decay = jnp.where(causal[None, :, :, None, None], jnp.exp(jnp.minimum(diff, 0.0)), 0.0)
        A = jnp.einsum('bihd,bjhd,bijhd->bhij', qc, kc, decay)
        o_intra = jnp.einsum('bhij,bjhv->bihv', A, vc)
        b_last = b[:, -1]
        S_new = (jnp.exp(b_last)[..., None] * S
                 + jnp.einsum('bjhd,bjhv->bhdv', kc * jnp.exp(b_last[:, None] - b), vc))
        return S_new, o_inter + o_intra

    S0 = jnp.zeros((B, G_HEADS, G_KEY_DIM, G_VAL_DIM), jnp.float32)
    qs = q * (G_KEY_DIM ** -0.5)
    _, o = lax.scan(step, S0, (to_chunks(qs), to_chunks(k), to_chunks(v), to_chunks(gk)))
    o = jnp.moveaxis(o, 0, 1).reshape(B, nC * CHUNK, G_HEADS, G_VAL_DIM)[:, lead:]
    o = rms_norm_f32(o, norm_g).astype(og.dtype).reshape(B, L, G_HEADS * G_VAL_DIM)
    return o * jax.nn.silu(og)


def peer_ffn(x, w_pq, sub_keys, u_tab, v_tab):
    B, L, D = x.shape
    T = B * L
    nblk = -(-T // P_BLOCK)
    xt = jnp.pad(x.reshape(T, D), ((0, nblk * P_BLOCK - T), (0, 0))).reshape(nblk, P_BLOCK, D)

    def one_block(xb):
        qh = (xb @ w_pq).reshape(P_BLOCK, P_HEADS, 2, P_QDIM // 2)
        s = jnp.einsum('thcd,hcnd->thcn', qh, sub_keys).astype(jnp.float32)
        s1, i1 = lax.top_k(s[:, :, 0], P_TOPK)
        s2, i2 = lax.top_k(s[:, :, 1], P_TOPK)
        cand = (s1[..., :, None] + s2[..., None, :]).reshape(P_BLOCK, P_HEADS, P_TOPK * P_TOPK)
        cidx = (i1[..., :, None] * P_NKEYS + i2[..., None, :]).reshape(P_BLOCK, P_HEADS, P_TOPK * P_TOPK)
        top, pos = lax.top_k(cand, P_TOPK)
        eidx = jnp.take_along_axis(cidx, pos, axis=-1).reshape(P_BLOCK, P_HEADS * P_TOPK)
        g = jax.nn.softmax(top, axis=-1).reshape(P_BLOCK, P_HEADS * P_TOPK).astype(xb.dtype)
        act = jax.nn.gelu(jnp.einsum('td,tkd->tk', xb, u_tab[eidx]))
        return jnp.einsum('tk,tkd->td', g * act, v_tab[eidx])

    y = lax.map(one_block, xt).reshape(nblk * P_BLOCK, D)[:T]
    return y.reshape(B, L, D)


def setup_inputs(seed: int = 0) -> dict:
    key = jax.random.key(seed)
    ks = jax.random.split(key, 20)
    f32 = jnp.float32

    def nrm(k, shape, scale):
        return jax.random.normal(k, shape, f32) * scale

    return {
        'x': nrm(ks[0], (BATCH, SEQ, D_MODEL), 1.0),
        'meta_tokens': nrm(ks[1], (N_META, D_MODEL), 1.0),
        'ln0_g': 1.0 + nrm(ks[2], (D_MODEL,), 0.02),
        'ln0_b': nrm(ks[3], (D_MODEL,), 0.02),
        'rel_bias': nrm(ks[4], (REL_BUCKETS, A_HEADS), 0.5),
        'w_in': nrm(ks[5], (DEPTH, D_MODEL, IN_COLS), D_MODEL ** -0.5),
        'w_uk': nrm(ks[6], (DEPTH, A_HEADS, A_KV_RANK, A_HEAD_DIM), A_HEAD_DIM ** -0.5),
        'w_uv': nrm(ks[7], (DEPTH, A_HEADS, A_KV_RANK, A_HEAD_DIM), A_KV_RANK ** -0.5),
        'w_gk2': nrm(ks[8], (DEPTH, G_GATE_RANK, G_HEADS * G_KEY_DIM), G_GATE_RANK ** -0.5),
        'b_gk': nrm(ks[9], (DEPTH, G_HEADS * G_KEY_DIM), 0.1),
        'gla_norm_g': 1.0 + nrm(ks[10], (DEPTH, G_VAL_DIM), 0.02),
        'w_out': nrm(ks[11], (DEPTH, MIX_WIDTH, D_MODEL), BETA * MIX_WIDTH ** -0.5),
        'ln1_g': 1.0 + nrm(ks[12], (DEPTH, D_MODEL), 0.02),
        'ln1_b': nrm(ks[13], (DEPTH, D_MODEL), 0.02),
        'w_pq': nrm(ks[14], (DEPTH, D_MODEL, P_HEADS * P_QDIM), D_MODEL ** -0.5),
        'sub_keys': nrm(ks[15], (DEPTH, P_HEADS, 2, P_NKEYS, P_QDIM // 2), (P_QDIM // 2) ** -0.5),
        'u_tab': nrm(ks[16], (DEPTH, P_NEXPERTS, D_MODEL), D_MODEL ** -0.5),
        'v_tab': nrm(ks[17], (DEPTH, P_NEXPERTS, D_MODEL), BETA * P_HEADS ** -0.5),
        'ln2_g': 1.0 + nrm(ks[18], (DEPTH, D_MODEL), 0.02),
        'ln2_b': nrm(ks[19], (DEPTH, D_MODEL), 0.02),
    }


def reference(x, meta_tokens, ln0_g, ln0_b, rel_bias, w_in, w_uk, w_uv, w_gk2, b_gk,
              gla_norm_g, w_out, ln1_g, ln1_b, w_pq, sub_keys, u_tab, v_tab, ln2_g, ln2_b):
    B, S, D = x.shape
    L = S + N_META
    topk = min(TOPK_MAX, S // 4)
    offsets = []
    acc = 0
    for w in SPLITS[:-1]:
        acc += w
        offsets.append(acc)
    meta = jnp.broadcast_to(meta_tokens[None].astype(x.dtype), (B, N_META, D))
    h = layer_norm(jnp.concatenate([meta, x], axis=1), ln0_g, ln0_b)
    for l in range(DEPTH):
        proj = h @ w_in[l]
        a_q, a_ckv, i_q, i_k, i_w, g_q, g_k, g_v, g_r, g_o = jnp.split(proj, offsets, axis=-1)
        y_a = dsa_mixer(a_q.reshape(B, L, A_HEADS, A_HEAD_DIM), a_ckv,
                        i_q.reshape(B, L, IDX_HEADS, IDX_DIM), i_k, i_w,
                        w_uk[l], w_uv[l], rel_bias, topk)
        gk = jax.nn.log_sigmoid((g_r @ w_gk2[l] + b_gk[l]).astype(jnp.float32)) / G_GATE_NORM
        y_b = gla_mixer(g_q.reshape(B, L, G_HEADS, G_KEY_DIM), g_k.reshape(B, L, G_HEADS, G_KEY_DIM),
                        g_v.reshape(B, L, G_HEADS, G_VAL_DIM), gk.reshape(B, L, G_HEADS, G_KEY_DIM),
                        g_o, gla_norm_g[l])
        mix = jnp.concatenate([y_a, y_b], axis=-1) @ w_out[l]
        h = layer_norm(ALPHA * h + mix, ln1_g[l], ln1_b[l])
        h = layer_norm(ALPHA * h + peer_ffn(h, w_pq[l], sub_keys[l], u_tab[l], v_tab[l]), ln2_g[l], ln2_b[l])
    return h[:, N_META:]
```

```python
import functools
import math

import numpy as np
import jax
import jax.numpy as jnp
from jax import lax
from jax.experimental import pallas as pl
from jax.experimental.pallas import tpu as pltpu

F32 = jnp.float32
BF16 = jnp.bfloat16
I32 = jnp.int32

N_META = 16
CHUNK = 64
EPS = 1e-5
DEPTH = 1
ALPHA = (2 * DEPTH) ** 0.25
A_HEADS = 8
A_HEAD_DIM = 128
A_KV_RANK = 256
IDX_HEADS = 16
IDX_DIM = 64
TOPK_MAX = 256
Q_BLOCK = 128
REL_BUCKETS = 32
REL_MAX_DIST = 128
G_HEADS = 4
G_VAL_DIM = 256
G_KEY_DIM = 128
G_GATE_RANK = 16
G_GATE_NORM = 16.0
P_HEADS = 8
P_NKEYS = 128
P_QDIM = 256
P_TOPK = 16

LANES = 128
SUBLANES = 8
VMEM_LIMIT = 56 * 1024 * 1024

C_AQ, C_IQ, C_GV, C_GO = 0, 1024, 2048, 3072
C_GQ, C_GK = 4096, 4608
C_CKV = 5120
C_IDX = 5376
C_GR = 5504
N_COLS = 5632

INT_MIN = -2 ** 31
NT_DIMS = (((1,), (1,)), ((), ()))
TN_DIMS = (((0,), (0,)), ((), ()))


def _cparams(*sem):
    return pltpu.CompilerParams(dimension_semantics=sem, vmem_limit_bytes=VMEM_LIMIT)


def _layer_norm(x, g, b):
    mu = jnp.mean(x, axis=-1, keepdims=True)
    xc = x - mu
    var = jnp.mean(xc * xc, axis=-1, keepdims=True)
    return xc * lax.rsqrt(var + EPS) * g + b


def _ln_proj_kernel(x_ref, g_ref, b_ref, w_ref, h_ref, p_ref, hb_ref):
    @pl.when(pl.program_id(1) == 0)
    def _():
        y = _layer_norm(x_ref[...], g_ref[...], b_ref[...])
        h_ref[...] = y
        hb_ref[...] = y.astype(BF16)

    p_ref[...] = jnp.dot(hb_ref[...], w_ref[...], preferred_element_type=F32)


def _ln_proj(x2, g, b, wcat, tm, tn):
    m, d = x2.shape
    n = wcat.shape[1]
    return pl.pallas_call(
        _ln_proj_kernel,
        out_shape=(jax.ShapeDtypeStruct((m, d), F32), jax.ShapeDtypeStruct((m, n), F32)),
        grid=(m // tm, n // tn),
        in_specs=[pl.BlockSpec((tm, d), lambda i, j: (i, 0)),
                  pl.BlockSpec((1, d), lambda i, j: (0, 0)),
                  pl.BlockSpec((1, d), lambda i, j: (0, 0)),
                  pl.BlockSpec((d, tn), lambda i, j: (0, j))],
        out_specs=(pl.BlockSpec((tm, d), lambda i, j: (i, 0)),
                   pl.BlockSpec((tm, tn), lambda i, j: (i, j))),
        scratch_shapes=[pltpu.VMEM((tm, d), BF16)],
        compiler_params=_cparams("parallel", "arbitrary"),
        name="ln_proj",
    )(x2, g, b, wcat)


N_PAT = 5


def _t5_bucket(rel):
    nb = REL_BUCKETS // 2
    max_exact = nb // 2
    base = jnp.where(rel > 0, nb, 0)
    n = jnp.abs(rel)
    nf = jnp.maximum(n, 1).astype(jnp.float32)
    large = max_exact + (jnp.log(nf / max_exact) / math.log(REL_MAX_DIST / max_exact)
                         * (nb - max_exact)).astype(jnp.int32)
    large = jnp.minimum(large, nb - 1)
    return base + jnp.where(n < max_exact, n, large)


def _bias_patterns():
    i = jnp.arange(Q_BLOCK, dtype=I32)[:, None]
    j = jnp.arange(LANES, dtype=I32)[None, :]
    rels = [j - i, j - i - LANES, j - i - 2 * LANES, j - N_META - i, j - N_META - i - LANES]
    return jnp.stack([_t5_bucket(r) for r in rels])


def _check_bucket_saturation(max_dist):
    nb = REL_BUCKETS // 2
    max_exact = nb // 2
    n = np.arange(LANES + 1, max_dist + 1, dtype=np.float64)
    large = max_exact + np.floor(np.log(n / max_exact) / math.log(REL_MAX_DIST / max_exact) * (nb - max_exact))
    assert np.all(large >= nb - 1 + 0.5), "relative-position buckets do not saturate within one key tile"


def _bias_tab_kernel(bk_ref, rb_ref, o_ref):
    h = pl.program_id(0)
    for p in range(N_PAT):
        bk = bk_ref[p]
        acc = jnp.zeros((Q_BLOCK, LANES), F32)
        for b in range(REL_BUCKETS):
            acc = jnp.where(bk == b, rb_ref[b, h], acc)
        o_ref[0, p] = acc


def _bias_tab(rel_bias):
    return pl.pallas_call(
        _bias_tab_kernel,
        out_shape=jax.ShapeDtypeStruct((A_HEADS, N_PAT, Q_BLOCK, LANES), F32),
        grid=(A_HEADS,),
        in_specs=[pl.BlockSpec((N_PAT, Q_BLOCK, LANES), lambda h: (0, 0, 0)),
                  pl.BlockSpec(memory_space=pltpu.SMEM)],
        out_specs=pl.BlockSpec((1, N_PAT, Q_BLOCK, LANES), lambda h: (h, 0, 0, 0)),
        compiler_params=_cparams("arbitrary"),
        name="bias_tab",
    )(_bias_patterns(), rel_bias)


def _dsa_kernel(aq_ref, iq_ref, idxq_ref, ckv_ref, idxk_ref, ckvm_ref, idxm_ref, wuk_ref, wuv_ref,
                bt_ref, o_ref,
                ik_scr, ckvb_scr, iqs_scr, l_scr, key_scr, el_scr, madd_scr, ql_scr, lg_scr, p_scr,
                den_scr, *, q_lo, width, topk):
    nt = 1 + width // LANES
    wt = nt * LANES
    qi = pl.program_id(1) + q_lo

    @pl.when(pl.program_id(1) == 0)
    def _stage_keys():
        ik_scr[0:LANES, :] = idxm_ref[:, 0:IDX_DIM].astype(BF16)
        ik_scr[LANES:wt, :] = idxk_ref[0, :, 0:IDX_DIM].astype(BF16)
        ckvb_scr[0:LANES, :] = ckvm_ref[...].astype(BF16)
        ckvb_scr[LANES:wt, :] = ckv_ref[0].astype(BF16)

    row = lax.broadcasted_iota(I32, (Q_BLOCK, LANES), 0)
    lane = lax.broadcasted_iota(I32, (Q_BLOCK, LANES), 1)
    kb = ((qi * Q_BLOCK + row) // CHUNK + 1) * CHUNK

    iq = iq_ref[0].astype(BF16)
    for h in range(IDX_HEADS):
        iqs_scr[h * Q_BLOCK:(h + 1) * Q_BLOCK, :] = iq[:, h * IDX_DIM:(h + 1) * IDX_DIM]
    iw = idxq_ref[0][:, IDX_DIM:IDX_DIM + IDX_HEADS] * (IDX_HEADS ** -0.5 * IDX_DIM ** -0.5)
    iw_b = [jnp.broadcast_to(iw[:, h:h + 1], (Q_BLOCK, LANES)) for h in range(IDX_HEADS)]
    for kt in range(nt):
        l_scr[...] = lax.dot_general(iqs_scr[...], ik_scr[kt * LANES:(kt + 1) * LANES, :], NT_DIMS,
                                     preferred_element_type=F32)
        acc = jnp.zeros((Q_BLOCK, LANES), F32)
        for h in range(IDX_HEADS):
            acc = acc + jnp.maximum(l_scr[h * Q_BLOCK:(h + 1) * Q_BLOCK, :], 0.0) * iw_b[h]
        adm = (lane < N_META) if kt == 0 else (lane + (kt - 1) * LANES < kb)
        bits = pltpu.bitcast(acc + 0.0, I32)
        key = jnp.where(bits < 0, bits ^ 0x7FFFFFFF, bits)
        key_scr[:, kt * LANES:(kt + 1) * LANES] = jnp.where(adm, key, INT_MIN)

    def count(pred):
        acc = jnp.zeros((Q_BLOCK, LANES), F32)
        for kt in range(nt):
            acc = acc + jnp.where(pred(kt), 1.0, 0.0)
        return jnp.sum(acc, axis=1, keepdims=True)

    def key_tile(kt):
        return key_scr[:, kt * LANES:(kt + 1) * LANES]

    def val_step(it, thr):
        cand = thr + jnp.left_shift(jnp.int32(1), 31 - it)
        cnt = count(lambda kt: key_tile(kt) >= cand)
        return jnp.where(cnt >= topk, cand, thr)

    thr = lax.fori_loop(0, 32, val_step, jnp.full((Q_BLOCK, 1), INT_MIN, I32))
    need = topk - count(lambda kt: key_tile(kt) > thr)
    big = jnp.int32(2 ** 30)
    for kt in range(nt):
        el_scr[:, kt * LANES:(kt + 1) * LANES] = jnp.where(key_tile(kt) == thr, lane + kt * LANES, big)

    def el_tile(kt):
        return el_scr[:, kt * LANES:(kt + 1) * LANES]

    nbits = max(1, int(wt - 1).bit_length())

    def pos_step(it, x):
        cand = x + jnp.left_shift(jnp.int32(1), nbits - 1 - it)
        cnt = count(lambda kt: el_tile(kt) < cand)
        return jnp.where(cnt < need, cand, x)

    xpos = lax.fori_loop(0, nbits, pos_step, jnp.zeros((Q_BLOCK, 1), I32))
    for kt in range(nt):
        k = key_tile(kt)
        sel = ((k > thr) | (el_tile(kt) <= xpos)) & (k != INT_MIN)
        madd_scr[:, kt * LANES:(kt + 1) * LANES] = jnp.where(sel, 0.0, -jnp.inf)

    aq = aq_ref[0].astype(BF16)
    for h in range(A_HEADS):
        qh = lax.dot_general(aq[:, h * A_HEAD_DIM:(h + 1) * A_HEAD_DIM], wuk_ref[h], NT_DIMS,
                             preferred_element_type=F32) * (A_HEAD_DIM ** -0.5)
        ql_scr[h * Q_BLOCK:(h + 1) * Q_BLOCK, :] = qh.astype(BF16)
    for kt in range(nt):
        lg_scr[:, kt * LANES:(kt + 1) * LANES] = lax.dot_general(
            ql_scr[...], ckvb_scr[kt * LANES:(kt + 1) * LANES, :], NT_DIMS, preferred_element_type=F32)

    def head_body(h, carry):
        r0 = pl.multiple_of(h * Q_BLOCK, Q_BLOCK)
        rows = pl.ds(r0, Q_BLOCK)
        m = jnp.full((Q_BLOCK, LANES), -jnp.inf, F32)
        for kt in range(nt):
            cols = slice(kt * LANES, (kt + 1) * LANES)
            pat = (3 + jnp.minimum(qi, 1)) if kt == 0 else jnp.clip(qi - (kt - 1), 0, 2)
            l = lg_scr[rows, cols] + bt_ref[h, pat] + madd_scr[:, cols]
            lg_scr[rows, cols] = l
            m = jnp.maximum(m, l)
        mrow = jnp.max(m, axis=1, keepdims=True)
        ssum = jnp.zeros((Q_BLOCK, LANES), F32)
        for kt in range(nt):
            cols = slice(kt * LANES, (kt + 1) * LANES)
            e = jnp.exp(lg_scr[rows, cols] - mrow)
            ssum = ssum + e
            p_scr[rows, cols] = e.astype(BF16)
        den_scr[rows, :] = jnp.broadcast_to(jnp.sum(ssum, axis=1, keepdims=True), (Q_BLOCK, LANES))
        return carry

    lax.fori_loop(0, A_HEADS, head_body, 0)
    o = jnp.dot(p_scr[...], ckvb_scr[...], preferred_element_type=F32)
    for h in range(A_HEADS):
        rs = slice(h * Q_BLOCK, (h + 1) * Q_BLOCK)
        oh = o[rs] / den_scr[rs, 0:1]
        o_ref[0, :, h * A_HEAD_DIM:(h + 1) * A_HEAD_DIM] = jnp.dot(
            oh.astype(BF16), wuv_ref[h], preferred_element_type=F32)


def _dsa(proj3, ckv_meta, idx_meta, wuk, wuv, btab, q_lo, n_q, width, topk):
    bsz, s, _ = proj3.shape
    nt = 1 + width // LANES
    wt = nt * LANES
    hq = A_HEADS * Q_BLOCK
    kern = functools.partial(_dsa_kernel, q_lo=q_lo, width=width, topk=topk)
    return pl.pallas_call(
        kern,
        out_shape=jax.ShapeDtypeStruct((bsz, n_q * Q_BLOCK, A_HEADS * A_HEAD_DIM), F32),
        grid=(bsz, n_q),
        in_specs=[
            pl.BlockSpec((1, Q_BLOCK, 1024), lambda b, q: (b, q + q_lo, C_AQ // 1024)),
            pl.BlockSpec((1, Q_BLOCK, 1024), lambda b, q: (b, q + q_lo, C_IQ // 1024)),
            pl.BlockSpec((1, Q_BLOCK, LANES), lambda b, q: (b, q + q_lo, C_IDX // LANES)),
            pl.BlockSpec((1, width, A_KV_RANK), lambda b, q: (b, 0, C_CKV // A_KV_RANK)),
            pl.BlockSpec((1, width, LANES), lambda b, q: (b, 0, C_IDX // LANES)),
            pl.BlockSpec((LANES, A_KV_RANK), lambda b, q: (0, 0)),
            pl.BlockSpec((LANES, LANES), lambda b, q: (0, 0)),
            pl.BlockSpec((A_HEADS, A_KV_RANK, A_HEAD_DIM), lambda b, q: (0, 0, 0)),
            pl.BlockSpec((A_HEADS, A_KV_RANK, A_HEAD_DIM), lambda b, q: (0, 0, 0)),
            pl.BlockSpec((A_HEADS, N_PAT, Q_BLOCK, LANES), lambda b, q: (0, 0, 0, 0)),
        ],
        out_specs=pl.BlockSpec((1, Q_BLOCK, A_HEADS * A_HEAD_DIM), lambda b, q: (b, q, 0)),
        scratch_shapes=[
            pltpu.VMEM((wt, IDX_DIM), BF16),
            pltpu.VMEM((wt, A_KV_RANK), BF16),
            pltpu.VMEM((IDX_HEADS * Q_BLOCK, IDX_DIM), BF16),
            pltpu.VMEM((IDX_HEADS * Q_BLOCK, LANES), F32),
            pltpu.VMEM((Q_BLOCK, wt), I32),
            pltpu.VMEM((Q_BLOCK, wt), I32),
            pltpu.VMEM((Q_BLOCK, wt), F32),
            pltpu.VMEM((hq, A_KV_RANK), BF16),
            pltpu.VMEM((hq, wt), F32),
            pltpu.VMEM((hq, wt), BF16),
            pltpu.VMEM((hq, LANES), F32),
        ],
        compiler_params=_cparams("parallel", "arbitrary"),
        name=f"dsa_w{width}",
    )(proj3, proj3, proj3, proj3, proj3, ckv_meta, idx_meta, wuk, wuv, btab)


def _gla_kernel(gq_ref, gk_ref, gv_ref, go_ref, gr_ref, gkm_ref, gvm_ref, grm_ref, w2_ref, bg_ref,
                ng_ref, o_ref, st_scr, kk_scr, bb_scr):
    w2 = w2_ref[...].astype(BF16)

    def log_decay(gr):
        z = jnp.dot(gr.astype(BF16), w2, preferred_element_type=F32) + bg_ref[...]
        return (jnp.minimum(z, 0.0) - jnp.log(1.0 + jnp.exp(-jnp.abs(z)))) * (1.0 / G_GATE_NORM)

    def cumsum_rows(g):
        n = g.shape[0]
        tri = (lax.broadcasted_iota(I32, (n, n), 0) >= lax.broadcasted_iota(I32, (n, n), 1)).astype(F32)
        return jnp.dot(tri, g, precision=lax.Precision.HIGHEST, preferred_element_type=F32)

    @pl.when(pl.program_id(1) == 0)
    def _init_state():
        bm = cumsum_rows(log_decay(grm_ref[:, 0:G_GATE_RANK]))
        kd = gkm_ref[...] * jnp.exp(bm[N_META - 1:N_META, :] - bm)
        for h in range(G_HEADS):
            st_scr[h] = lax.dot_general(
                gvm_ref[:, h * G_VAL_DIM:(h + 1) * G_VAL_DIM].astype(BF16),
                kd[:, h * G_KEY_DIM:(h + 1) * G_KEY_DIM].astype(BF16), TN_DIMS,
                preferred_element_type=F32)

    b = cumsum_rows(log_decay(gr_ref[0][:, 0:G_GATE_RANK]))
    row = lax.broadcasted_iota(I32, (CHUNK, CHUNK), 0)
    col = lax.broadcasted_iota(I32, (CHUNK, CHUNK), 1)
    for h in range(G_HEADS):
        ks = slice(h * G_KEY_DIM, (h + 1) * G_KEY_DIM)
        vs = slice(h * G_VAL_DIM, (h + 1) * G_VAL_DIM)
        q = gq_ref[0][:, ks] * (G_KEY_DIM ** -0.5)
        k = gk_ref[0][:, ks]
        v = gv_ref[0][:, vs].astype(BF16)
        bh = b[:, ks]
        st = st_scr[h]
        o = lax.dot_general((q * jnp.exp(bh)).astype(BF16), st.astype(BF16), NT_DIMS,
                            preferred_element_type=F32)
        kk_scr[...] = k
        bb_scr[...] = bh

        def col_group(jg, a):
            j0 = pl.multiple_of(jg * SUBLANES, SUBLANES)
            k8 = kk_scr[pl.ds(j0, SUBLANES), :]
            b8 = bb_scr[pl.ds(j0, SUBLANES), :]
            for r in range(SUBLANES):
                w = q * k8[r:r + 1, :] * jnp.exp(jnp.minimum(bh - b8[r:r + 1, :], 0.0))
                a = jnp.where(col == j0 + r, jnp.sum(w, axis=1, keepdims=True), a)
            return a

        a = lax.fori_loop(0, CHUNK // SUBLANES, col_group, jnp.zeros((CHUNK, CHUNK), F32))
        a = jnp.where(row >= col, a, 0.0)
        o = o + jnp.dot(a.astype(BF16), v, preferred_element_type=F32)
        bl = bh[CHUNK - 1:CHUNK, :]
        kd = k * jnp.exp(bl - bh)
        st_scr[h] = st * jnp.exp(bl) + lax.dot_general(v, kd.astype(BF16), TN_DIMS,
                                                       preferred_element_type=F32)
        on = o * lax.rsqrt(jnp.mean(o * o, axis=1, keepdims=True) + EPS) * ng_ref[...]
        og = go_ref[0][:, vs]
        o_ref[0, :, vs] = on * (og * (1.0 / (1.0 + jnp.exp(-og))))


def _gla(proj3, proj_meta, w_gk2, b_gk, norm_g):
    bsz, s, _ = proj3.shape
    hk = G_HEADS * G_KEY_DIM
    hv = G_HEADS * G_VAL_DIM
    gkm = proj_meta[:, C_GK:C_GK + hk]
    gvm = proj_meta[:, C_GV:C_GV + hv]
    grm = proj_meta[:, C_GR:C_GR + LANES]
    full = lambda shp: pl.BlockSpec(shp, lambda b, c: (0,) * len(shp))
    return pl.pallas_call(
        _gla_kernel,
        out_shape=jax.ShapeDtypeStruct((bsz, s, hv), F32),
        grid=(bsz, s // CHUNK),
        in_specs=[
            pl.BlockSpec((1, CHUNK, hk), lambda b, c: (b, c, C_GQ // hk)),
            pl.BlockSpec((1, CHUNK, hk), lambda b, c: (b, c, C_GK // hk)),
            pl.BlockSpec((1, CHUNK, hv), lambda b, c: (b, c, C_GV // hv)),
            pl.BlockSpec((1, CHUNK, hv), lambda b, c: (b, c, C_GO // hv)),
            pl.BlockSpec((1, CHUNK, LANES), lambda b, c: (b, c, C_GR // LANES)),
            full((N_META, hk)), full((N_META, hv)), full((N_META, LANES)),
            full((G_GATE_RANK, hk)), full((1, hk)), full((1, G_VAL_DIM)),
        ],
        out_specs=pl.BlockSpec((1, CHUNK, hv), lambda b, c: (b, c, 0)),
        scratch_shapes=[pltpu.VMEM((G_HEADS, G_VAL_DIM, G_KEY_DIM), F32),
                        pltpu.VMEM((CHUNK, G_KEY_DIM), F32),
                        pltpu.VMEM((CHUNK, G_KEY_DIM), F32)],
        compiler_params=_cparams("parallel", "arbitrary"),
        name="gla",
    )(proj3, proj3, proj3, proj3, proj3, gkm, gvm, grm, w_gk2, b_gk, norm_g)


def _out_ln_kernel(ya_ref, yb_ref, h_ref, wa_ref, wb_ref, g_ref, b_ref, o_ref):
    mix = (jnp.dot(ya_ref[...].astype(BF16), wa_ref[...], preferred_element_type=F32)
           + jnp.dot(yb_ref[...].astype(BF16), wb_ref[...], preferred_element_type=F32))
    o_ref[...] = _layer_norm(ALPHA * h_ref[...] + mix, g_ref[...], b_ref[...])


def _out_ln(ya, yb, h, wa, wb, g, b, tm):
    m, d = h.shape
    ka, kb = ya.shape[1], yb.shape[1]
    return pl.pallas_call(
        _out_ln_kernel,
        out_shape=jax.ShapeDtypeStruct((m, d), F32),
        grid=(m // tm,),
        in_specs=[pl.BlockSpec((tm, ka), lambda i: (i, 0)),
                  pl.BlockSpec((tm, kb), lambda i: (i, 0)),
                  pl.BlockSpec((tm, d), lambda i: (i, 0)),
                  pl.BlockSpec((ka, d), lambda i: (0, 0)),
                  pl.BlockSpec((kb, d), lambda i: (0, 0)),
                  pl.BlockSpec((1, d), lambda i: (0, 0)),
                  pl.BlockSpec((1, d), lambda i: (0, 0))],
        out_specs=pl.BlockSpec((tm, d), lambda i: (i, 0)),
        compiler_params=_cparams("parallel"),
        name="out_ln",
    )(ya, yb, h, wa, wb, g, b)


def _top_rows(s, n_rows, k):
    rid = lax.broadcasted_iota(I32, (n_rows, LANES), 0)
    slot = lax.broadcasted_iota(I32, (k, LANES), 0)
    vals = jnp.zeros((k, LANES), F32)
    idxs = jnp.zeros((k, LANES), I32)
    for r in range(k):
        m = jnp.max(s, axis=0, keepdims=True)
        im = jnp.min(jnp.where(s == m, rid, n_rows), axis=0, keepdims=True)
        vals = jnp.where(slot == r, m, vals)
        idxs = jnp.where(slot == r, im, idxs)
        s = jnp.where(rid == im, -jnp.inf, s)
    return vals, idxs


def _route_kernel(h_ref, wq_ref, sk_ref, e_ref, g_ref, qt_scr, *, n_half):
    qt = lax.dot_general(wq_ref[...], h_ref[...].astype(BF16), NT_DIMS, preferred_element_type=F32)
    for half in range(n_half):
        qt_scr[half] = qt[:, half * LANES:(half + 1) * LANES].astype(BF16)
    n_cand = P_TOPK * P_TOPK
    cid = lax.broadcasted_iota(I32, (n_cand, LANES), 0)
    slot = lax.broadcasted_iota(I32, (P_TOPK, LANES), 0)
    half_dim = P_QDIM // 2

    def unit(u, carry):
        h = u // n_half
        half = u % n_half
        tops = []
        for c in range(2):
            d0 = pl.multiple_of((h * 2 + c) * half_dim, half_dim)
            s = jnp.dot(sk_ref[h, c], qt_scr[half, pl.ds(d0, half_dim), :], preferred_element_type=F32)
            tops.append(_top_rows(s, P_NKEYS, P_TOPK))
        (s1, i1), (s2, i2) = tops
        cand = jnp.concatenate([s1[a:a + 1, :] + s2 for a in range(P_TOPK)], axis=0)
        cidx = jnp.concatenate([i1[a:a + 1, :] * P_NKEYS + i2 for a in range(P_TOPK)], axis=0)
        top, pos = _top_rows(cand, n_cand, P_TOPK)
        eidx = jnp.zeros((P_TOPK, LANES), I32)
        for r in range(P_TOPK):
            er = jnp.max(jnp.where(cid == pos[r:r + 1, :], cidx, -1), axis=0, keepdims=True)
            eidx = jnp.where(slot == r, er, eidx)
        ex = jnp.exp(top - top[0:1, :])
        gate = ex / jnp.sum(ex, axis=0, keepdims=True)
        r0 = pl.multiple_of(h * P_TOPK, P_TOPK)
        e_ref[half, pl.ds(r0, P_TOPK), :] = eidx
        g_ref[half, pl.ds(r0, P_TOPK), :] = gate
        return carry

    lax.fori_loop(0, P_HEADS * n_half, unit, 0)


def _route(h1, wq_t, sk, tm):
    m, d = h1.shape
    n_half = tm // LANES
    nk = P_HEADS * P_TOPK
    kern = functools.partial(_route_kernel, n_half=n_half)
    return pl.pallas_call(
        kern,
        out_shape=(jax.ShapeDtypeStruct((m // LANES, nk, LANES), I32),
                   jax.ShapeDtypeStruct((m // LANES, nk, LANES), F32)),
        grid=(m // tm,),
        in_specs=[pl.BlockSpec((tm, d), lambda i: (i, 0)),
                  pl.BlockSpec(wq_t.shape, lambda i: (0, 0)),
                  pl.BlockSpec(sk.shape, lambda i: (0, 0, 0, 0))],
        out_specs=(pl.BlockSpec((n_half, nk, LANES), lambda i: (i, 0, 0)),
                   pl.BlockSpec((n_half, nk, LANES), lambda i: (i, 0, 0))),
        scratch_shapes=[pltpu.VMEM((n_half, wq_t.shape[0], LANES), BF16)],
        compiler_params=_cparams("parallel"),
        name="route",
    )(h1, wq_t, sk)


def _sublane_sums(rs):
    sub = lax.broadcasted_iota(I32, (SUBLANES, LANES), 0)
    lo4 = sub < 4
    lo2 = (sub % 4) < 2
    even = (sub % 2) == 0

    def lvl1(a, b):
        return jnp.where(lo4, a, b) + pltpu.roll(jnp.where(lo4, b, a), 4, 0)

    def lvl(a, b, mask, d):
        return jnp.where(mask, a + pltpu.roll(a, SUBLANES - d, 0), b + pltpu.roll(b, d, 0))

    c04, c26, c15, c37 = lvl1(rs[0], rs[4]), lvl1(rs[2], rs[6]), lvl1(rs[1], rs[5]), lvl1(rs[3], rs[7])
    return lvl(lvl(c04, c26, lo2, 2), lvl(c15, c37, lo2, 2), even, 1)


def _gelu_tanh(x):
    return 0.5 * x * (1.0 + jnp.tanh(math.sqrt(2.0 / math.pi) * (x + 0.044715 * (x * x * x))))


def _apply_kernel(idx_ref, idxn_ref, x_ref, gate_ref, g_ref, b_ref, uv_hbm, o_ref, buf, gb_scr, sem,
                  *, tt, nk, dsub):
    i = pl.program_id(0)
    n = pl.num_programs(0)
    slot = i % 2

    def issue(src_idx, dst_slot):
        def tok(t, carry):
            for k in range(nk):
                pltpu.make_async_copy(uv_hbm.at[src_idx[t, k]], buf.at[dst_slot, t * nk + k],
                                      sem.at[dst_slot]).start()
            return carry
        lax.fori_loop(0, tt, tok, 0)

    @pl.when(i == 0)
    def _():
        issue(idx_ref, 0)

    @pl.when(i + 1 < n)
    def _():
        issue(idxn_ref, 1 - slot)

    pltpu.make_async_copy(uv_hbm.at[pl.ds(0, tt * nk)], buf.at[slot], sem.at[slot]).wait()

    lane = lax.broadcasted_iota(I32, (nk, LANES), 1)
    lane0 = (i * tt) % LANES

    def tok(t, carry):
        x = x_ref[t]
        gcol = jnp.sum(jnp.where(lane == lane0 + t, gate_ref[0], 0.0), axis=1, keepdims=True)
        gb_scr[...] = jnp.broadcast_to(gcol, (nk, LANES))

        def group(g, y):
            base = t * nk + g * SUBLANES
            rs = []
            for k in range(SUBLANES):
                p = buf[slot, base + k, 0:dsub, :] * x
                r = p[0:SUBLANES]
                for j in range(1, dsub // SUBLANES):
                    r = r + p[j * SUBLANES:(j + 1) * SUBLANES]
                rs.append(r)
            act = jnp.sum(_sublane_sums(rs), axis=1, keepdims=True)
            g8 = gb_scr[pl.ds(pl.multiple_of(g * SUBLANES, SUBLANES), SUBLANES), :]
            c = _gelu_tanh(jnp.broadcast_to(act, (SUBLANES, LANES))) * g8
            for k in range(SUBLANES):
                y = y + buf[slot, base + k, dsub:2 * dsub, :] * c[k:k + 1, :]
            return y

        y = lax.fori_loop(0, nk // SUBLANES, group, jnp.zeros((dsub, LANES), F32))
        r = ALPHA * x + y
        cnt = float(dsub * LANES)
        mu = jnp.sum(jnp.sum(r, axis=1, keepdims=True), axis=0, keepdims=True) / cnt
        rc = r - mu
        var = jnp.sum(jnp.sum(rc * rc, axis=1, keepdims=True), axis=0, keepdims=True) / cnt
        o_ref[t] = rc * lax.rsqrt(var + EPS) * g_ref[...] + b_ref[...]
        return carry

    lax.fori_loop(0, tt, tok, 0)


def _apply(idx, h1_3, gates, g3, b3, uv3, tt):
    m, nk = idx.shape
    dsub = h1_3.shape[1]
    n_steps = m // tt
    kern = functools.partial(_apply_kernel, tt=tt, nk=nk, dsub=dsub)
    return pl.pallas_call(
        kern,
        out_shape=jax.ShapeDtypeStruct(h1_3.shape, F32),
        grid=(n_steps,),
        in_specs=[pl.BlockSpec((tt, nk), lambda i: (i, 0), memory_space=pltpu.SMEM),
                  pl.BlockSpec((tt, nk), lambda i: (jnp.minimum(i + 1, n_steps - 1), 0),
                               memory_space=pltpu.SMEM),
                  pl.BlockSpec((tt, dsub, LANES), lambda i: (i, 0, 0)),
                  pl.BlockSpec((1, nk, LANES), lambda i: (i * tt // LANES, 0, 0)),
                  pl.BlockSpec((dsub, LANES), lambda i: (0, 0)),
                  pl.BlockSpec((dsub, LANES), lambda i: (0, 0)),
                  pl.BlockSpec(memory_space=pl.ANY)],
        out_specs=pl.BlockSpec((tt, dsub, LANES), lambda i: (i, 0, 0)),
        scratch_shapes=[pltpu.VMEM((2, tt * nk, 2 * dsub, LANES), F32),
                        pltpu.VMEM((nk, LANES), F32),
                        pltpu.SemaphoreType.DMA((2,))],
        compiler_params=_cparams("arbitrary"),
        name="peer_apply",
    )(idx, idx, h1_3, gates, g3, b3, uv3)


def _regroup_w_in(w):
    d = w.shape[0]
    widths = (A_HEADS * A_HEAD_DIM, A_KV_RANK, IDX_HEADS * IDX_DIM, IDX_DIM, IDX_HEADS,
              G_HEADS * G_KEY_DIM, G_HEADS * G_KEY_DIM, G_HEADS * G_VAL_DIM, G_GATE_RANK,
              G_HEADS * G_VAL_DIM)
    offs = np.concatenate([[0], np.cumsum(widths)])
    a_q, a_ckv, i_q, i_k, i_w, g_q, g_k, g_v, g_r, g_o = [w[:, offs[n]:offs[n + 1]] for n in range(10)]
    z = lambda n: jnp.zeros((d, n), w.dtype)
    cat = jnp.concatenate([a_q, i_q, g_v, g_o, g_q, g_k, a_ckv,
                           i_k, i_w, z(LANES - IDX_DIM - IDX_HEADS),
                           g_r, z(LANES - G_GATE_RANK)], axis=1)
    assert cat.shape[1] == N_COLS
    return cat.astype(BF16)


def _dsa_groups(n_qblocks, n_groups):
    per = -(-n_qblocks // n_groups)
    return [(lo, min(per, n_qblocks - lo)) for lo in range(0, n_qblocks, per)]


def kernel(x, meta_tokens, ln0_g, ln0_b, rel_bias, w_in, w_uk, w_uv, w_gk2, b_gk, gla_norm_g, w_out,
           ln1_g, ln1_b, w_pq, sub_keys, u_tab, v_tab, ln2_g, ln2_b):
    bsz, s, d = x.shape
    t = bsz * s
    topk = min(TOPK_MAX, s // 4)
    assert s % Q_BLOCK == 0 and d % (SUBLANES * LANES) == 0
    _check_bucket_saturation(s + N_META)
    row2 = lambda v: v.reshape(1, -1)

    wcat = _regroup_w_in(w_in[0])
    h, proj = _ln_proj(x.reshape(t, d), row2(ln0_g), row2(ln0_b), wcat, tm=512, tn=512)
    _, proj_meta = _ln_proj(meta_tokens, row2(ln0_g), row2(ln0_b), wcat, tm=N_META, tn=512)
    proj3 = proj.reshape(bsz, s, N_COLS)

    pad_rows = lambda a: jnp.pad(a, ((0, LANES - N_META), (0, 0)))
    ckv_meta = pad_rows(proj_meta[:, C_CKV:C_CKV + A_KV_RANK])
    idx_meta = pad_rows(proj_meta[:, C_IDX:C_IDX + LANES])
    btab = _bias_tab(rel_bias)
    wuk = w_uk[0].astype(BF16)
    wuv = w_uv[0].astype(BF16)
    ya_parts = []
    for q_lo, n_q in _dsa_groups(s // Q_BLOCK, 4):
        width = (q_lo + n_q) * Q_BLOCK
        ya_parts.append(_dsa(proj3, ckv_meta, idx_meta, wuk, wuv, btab, q_lo, n_q, width, topk))
    y_a = jnp.concatenate(ya_parts, axis=1).reshape(t, A_HEADS * A_HEAD_DIM)

    y_b = _gla(proj3, proj_meta, w_gk2[0], row2(b_gk[0]), row2(gla_norm_g[0])).reshape(t, -1)

    wo = w_out[0].astype(BF16)
    ka = A_HEADS * A_HEAD_DIM
    h1 = _out_ln(y_a, y_b, h, wo[:ka], wo[ka:], row2(ln1_g[0]), row2(ln1_b[0]), tm=256)

    wq_t = w_pq[0].T.astype(BF16)
    eidx, gates = _route(h1, wq_t, sub_keys[0].astype(BF16), tm=256)
    nk = P_HEADS * P_TOPK
    idx = eidx.transpose(0, 2, 1).reshape(t, nk)
    dsub = d // LANES
    uv3 = jnp.concatenate([u_tab[0].reshape(-1, dsub, LANES), v_tab[0].reshape(-1, dsub, LANES)], axis=1)
    out3 = _apply(idx, h1.reshape(t, dsub, LANES), gates, ln2_g[0].reshape(dsub, LANES),
                  ln2_b[0].reshape(dsub, LANES), uv3, tt=8)
    return out3.reshape(bsz, s, d)
```

```python
import functools
import math

import numpy as np
import jax
import jax.numpy as jnp
from jax import lax
from jax.experimental import pallas as pl
from jax.experimental.pallas import tpu as pltpu

F32 = jnp.float32
BF16 = jnp.bfloat16
I32 = jnp.int32

N_META = 16
CHUNK = 64
EPS = 1e-5
DEPTH = 1
ALPHA = (2 * DEPTH) ** 0.25
A_HEADS = 8
A_HEAD_DIM = 128
A_KV_RANK = 256
IDX_HEADS = 16
IDX_DIM = 64
TOPK_MAX = 256
Q_BLOCK = 128
REL_BUCKETS = 32
REL_MAX_DIST = 128
G_HEADS = 4
G_VAL_DIM = 256
G_KEY_DIM = 128
G_GATE_RANK = 16
G_GATE_NORM = 16.0
P_HEADS = 8
P_NKEYS = 128
P_QDIM = 256
P_TOPK = 16

LANES = 128
SUBLANES = 8
VMEM_LIMIT = 56 * 1024 * 1024

C_AQ, C_IQ, C_GV, C_GO = 0, 1024, 2048, 3072
C_GQ, C_GK = 4096, 4608
C_CKV = 5120
C_IDX = 5376
C_GR = 5504
N_COLS = 5632

INT_MIN = -2 ** 31
NT_DIMS = (((1,), (1,)), ((), ()))
TN_DIMS = (((0,), (0,)), ((), ()))


def _cparams(*sem):
    return pltpu.CompilerParams(dimension_semantics=sem, vmem_limit_bytes=VMEM_LIMIT)


def _layer_norm(x, g, b):
    mu = jnp.mean(x, axis=-1, keepdims=True)
    xc = x - mu
    var = jnp.mean(xc * xc, axis=-1, keepdims=True)
    return xc * lax.rsqrt(var + EPS) * g + b


def _ln_proj_kernel(x_ref, g_ref, b_ref, w_ref, h_ref, p_ref, hb_ref):
    @pl.when(pl.program_id(1) == 0)
    def _():
        y = _layer_norm(x_ref[...], g_ref[...], b_ref[...])
        h_ref[...] = y
        hb_ref[...] = y.astype(BF16)

    p_ref[...] = jnp.dot(hb_ref[...], w_ref[...], preferred_element_type=F32)


def _ln_proj(x2, g, b, wcat, tm, tn):
    m, d = x2.shape
    n = wcat.shape[1]
    return pl.pallas_call(
        _ln_proj_kernel,
        out_shape=(jax.ShapeDtypeStruct((m, d), F32), jax.ShapeDtypeStruct((m, n), F32)),
        grid=(m // tm, n // tn),
        in_specs=[pl.BlockSpec((tm, d), lambda i, j: (i, 0)),
                  pl.BlockSpec((1, d), lambda i, j: (0, 0)),
                  pl.BlockSpec((1, d), lambda i, j: (0, 0)),
                  pl.BlockSpec((d, tn), lambda i, j: (0, j))],
        out_specs=(pl.BlockSpec((tm, d), lambda i, j: (i, 0)),
                   pl.BlockSpec((tm, tn), lambda i, j: (i, j))),
        scratch_shapes=[pltpu.VMEM((tm, d), BF16)],
        compiler_params=_cparams("parallel", "arbitrary"),
        name="ln_proj",
    )(x2, g, b, wcat)


N_PAT = 5


def _t5_bucket(rel):
    nb = REL_BUCKETS // 2
    max_exact = nb // 2
    base = jnp.where(rel > 0, nb, 0)
    n = jnp.abs(rel)
    nf = jnp.maximum(n, 1).astype(jnp.float32)
    large = max_exact + (jnp.log(nf / max_exact) / math.log(REL_MAX_DIST / max_exact)
                         * (nb - max_exact)).astype(jnp.int32)
    large = jnp.minimum(large, nb - 1)
    return base + jnp.where(n < max_exact, n, large)


def _bias_patterns():
    i = jnp.arange(Q_BLOCK, dtype=I32)[:, None]
    j = jnp.arange(LANES, dtype=I32)[None, :]
    rels = [j - i, j - i - LANES, j - i - 2 * LANES, j - N_META - i, j - N_META - i - LANES]
    return jnp.stack([_t5_bucket(r) for r in rels])


def _check_bucket_saturation(max_dist):
    nb = REL_BUCKETS // 2
    max_exact = nb // 2
    n = np.arange(LANES + 1, max_dist + 1, dtype=np.float64)
    large = max_exact + np.floor(np.log(n / max_exact) / math.log(REL_MAX_DIST / max_exact) * (nb - max_exact))
    assert np.all(large >= nb - 1 + 0.5), "relative-position buckets do not saturate within one key tile"


def _bias_tab_kernel(bk_ref, rb_ref, o_ref):
    h = pl.program_id(0)
    for p in range(N_PAT):
        bk = bk_ref[p]
        acc = jnp.zeros((Q_BLOCK, LANES), F32)
        for b in range(REL_BUCKETS):
            acc = jnp.where(bk == b, rb_ref[b, h], acc)
        o_ref[0, p] = acc


def _bias_tab(rel_bias):
    return pl.pallas_call(
        _bias_tab_kernel,
        out_shape=jax.ShapeDtypeStruct((A_HEADS, N_PAT, Q_BLOCK, LANES), F32),
        grid=(A_HEADS,),
        in_specs=[pl.BlockSpec((N_PAT, Q_BLOCK, LANES), lambda h: (0, 0, 0)),
                  pl.BlockSpec(memory_space=pltpu.SMEM)],
        out_specs=pl.BlockSpec((1, N_PAT, Q_BLOCK, LANES), lambda h: (h, 0, 0, 0)),
        compiler_params=_cparams("arbitrary"),
        name="bias_tab",
    )(_bias_patterns(), rel_bias)


def _dsa_kernel(aq_ref, iq_ref, idxq_ref, ckv_ref, idxk_ref, ckvm_ref, idxm_ref, wuk_ref, wuv_ref,
                bt_ref, o_ref,
                ik_scr, ckvb_scr, iqs_scr, l_scr, key_scr, el_scr, madd_scr, ql_scr, lg_scr, p_scr,
                den_scr, *, q_lo, width, topk):
    nt = 1 + width // LANES
    wt = nt * LANES
    qi = pl.program_id(1) + q_lo

    @pl.when(pl.program_id(1) == 0)
    def _stage_keys():
        ik_scr[0:LANES, :] = idxm_ref[:, 0:IDX_DIM].astype(BF16)
        ik_scr[LANES:wt, :] = idxk_ref[0, :, 0:IDX_DIM].astype(BF16)
        ckvb_scr[0:LANES, :] = ckvm_ref[...].astype(BF16)
        ckvb_scr[LANES:wt, :] = ckv_ref[0].astype(BF16)

    row = lax.broadcasted_iota(I32, (Q_BLOCK, LANES), 0)
    lane = lax.broadcasted_iota(I32, (Q_BLOCK, LANES), 1)
    kb = ((qi * Q_BLOCK + row) // CHUNK + 1) * CHUNK

    iq = iq_ref[0].astype(BF16)
    for h in range(IDX_HEADS):
        iqs_scr[h * Q_BLOCK:(h + 1) * Q_BLOCK, :] = iq[:, h * IDX_DIM:(h + 1) * IDX_DIM]
    iw = idxq_ref[0][:, IDX_DIM:IDX_DIM + IDX_HEADS] * (IDX_HEADS ** -0.5 * IDX_DIM ** -0.5)
    iw_b = [jnp.broadcast_to(iw[:, h:h + 1], (Q_BLOCK, LANES)) for h in range(IDX_HEADS)]
    for kt in range(nt):
        l_scr[...] = lax.dot_general(iqs_scr[...], ik_scr[kt * LANES:(kt + 1) * LANES, :], NT_DIMS,
                                     preferred_element_type=F32)
        acc = jnp.zeros((Q_BLOCK, LANES), F32)
        for h in range(IDX_HEADS):
            acc = acc + jnp.maximum(l_scr[h * Q_BLOCK:(h + 1) * Q_BLOCK, :], 0.0) * iw_b[h]
        adm = (lane < N_META) if kt == 0 else (lane + (kt - 1) * LANES < kb)
        bits = pltpu.bitcast(acc + 0.0, I32)
        key = jnp.where(bits < 0, bits ^ 0x7FFFFFFF, bits)
        key_scr[:, kt * LANES:(kt + 1) * LANES] = jnp.where(adm, key, INT_MIN)

    def count(pred):
        acc = jnp.zeros((Q_BLOCK, LANES), F32)
        for kt in range(nt):
            acc = acc + jnp.where(pred(kt), 1.0, 0.0)
        return jnp.sum(acc, axis=1, keepdims=True)

    def key_tile(kt):
        return key_scr[:, kt * LANES:(kt + 1) * LANES]

    def val_step(it, thr):
        cand = thr + jnp.left_shift(jnp.int32(1), 31 - it)
        cnt = count(lambda kt: key_tile(kt) >= cand)
        return jnp.where(cnt >= topk, cand, thr)

    thr = lax.fori_loop(0, 32, val_step, jnp.full((Q_BLOCK, 1), INT_MIN, I32))
    need = topk - count(lambda kt: key_tile(kt) > thr)
    big = jnp.int32(2 ** 30)
    for kt in range(nt):
        el_scr[:, kt * LANES:(kt + 1) * LANES] = jnp.where(key_tile(kt) == thr, lane + kt * LANES, big)

    def el_tile(kt):
        return el_scr[:, kt * LANES:(kt + 1) * LANES]

    nbits = max(1, int(wt - 1).bit_length())

    def pos_step(it, x):
        cand = x + jnp.left_shift(jnp.int32(1), nbits - 1 - it)
        cnt = count(lambda kt: el_tile(kt) < cand)
        return jnp.where(cnt < need, cand, x)

    xpos = lax.fori_loop(0, nbits, pos_step, jnp.zeros((Q_BLOCK, 1), I32))
    for kt in range(nt):
        k = key_tile(kt)
        sel = ((k > thr) | (el_tile(kt) <= xpos)) & (k != INT_MIN)
        madd_scr[:, kt * LANES:(kt + 1) * LANES] = jnp.where(sel, 0.0, -jnp.inf)

    aq = aq_ref[0].astype(BF16)
    for h in range(A_HEADS):
        qh = lax.dot_general(aq[:, h * A_HEAD_DIM:(h + 1) * A_HEAD_DIM], wuk_ref[h], NT_DIMS,
                             preferred_element_type=F32) * (A_HEAD_DIM ** -0.5)
        ql_scr[h * Q_BLOCK:(h + 1) * Q_BLOCK, :] = qh.astype(BF16)
    for kt in range(nt):
        lg_scr[:, kt * LANES:(kt + 1) * LANES] = lax.dot_general(
            ql_scr[...], ckvb_scr[kt * LANES:(kt + 1) * LANES, :], NT_DIMS, preferred_element_type=F32)

    def head_body(h, carry):
        r0 = pl.multiple_of(h * Q_BLOCK, Q_BLOCK)
        rows = pl.ds(r0, Q_BLOCK)
        m = jnp.full((Q_BLOCK, LANES), -jnp.inf, F32)
        for kt in range(nt):
            cols = slice(kt * LANES, (kt + 1) * LANES)
            pat = (3 + jnp.minimum(qi, 1)) if kt == 0 else jnp.clip(qi - (kt - 1), 0, 2)
            l = lg_scr[rows, cols] + bt_ref[h, pat] + madd_scr[:, cols]
            lg_scr[rows, cols] = l
            m = jnp.maximum(m, l)
        mrow = jnp.max(m, axis=1, keepdims=True)
        ssum = jnp.zeros((Q_BLOCK, LANES), F32)
        for kt in range(nt):
            cols = slice(kt * LANES, (kt + 1) * LANES)
            e = jnp.exp(lg_scr[rows, cols] - mrow)
            ssum = ssum + e
            p_scr[rows, cols] = e.astype(BF16)
        den_scr[rows, :] = jnp.broadcast_to(jnp.sum(ssum, axis=1, keepdims=True), (Q_BLOCK, LANES))
        return carry

    lax.fori_loop(0, A_HEADS, head_body, 0)
    o = jnp.dot(p_scr[...], ckvb_scr[...], preferred_element_type=F32)
    for h in range(A_HEADS):
        rs = slice(h * Q_BLOCK, (h + 1) * Q_BLOCK)
        oh = o[rs] / den_scr[rs, 0:1]
        o_ref[0, :, h * A_HEAD_DIM:(h + 1) * A_HEAD_DIM] = jnp.dot(
            oh.astype(BF16), wuv_ref[h], preferred_element_type=F32)


def _dsa(proj3, ckv_meta, idx_meta, wuk, wuv, btab, q_lo, n_q, width, topk):
    bsz, s, _ = proj3.shape
    nt = 1 + width // LANES
    wt = nt * LANES
    hq = A_HEADS * Q_BLOCK
    kern = functools.partial(_dsa_kernel, q_lo=q_lo, width=width, topk=topk)
    return pl.pallas_call(
        kern,
        out_shape=jax.ShapeDtypeStruct((bsz, n_q * Q_BLOCK, A_HEADS * A_HEAD_DIM), F32),
        grid=(bsz, n_q),
        in_specs=[
            pl.BlockSpec((1, Q_BLOCK, 1024), lambda b, q: (b, q + q_lo, C_AQ // 1024)),
            pl.BlockSpec((1, Q_BLOCK, 1024), lambda b, q: (b, q + q_lo, C_IQ // 1024)),
            pl.BlockSpec((1, Q_BLOCK, LANES), lambda b, q: (b, q + q_lo, C_IDX // LANES)),
            pl.BlockSpec((1, width, A_KV_RANK), lambda b, q: (b, 0, C_CKV // A_KV_RANK)),
            pl.BlockSpec((1, width, LANES), lambda b, q: (b, 0, C_IDX // LANES)),
            pl.BlockSpec((LANES, A_KV_RANK), lambda b, q: (0, 0)),
            pl.BlockSpec((LANES, LANES), lambda b, q: (0, 0)),
            pl.BlockSpec((A_HEADS, A_KV_RANK, A_HEAD_DIM), lambda b, q: (0, 0, 0)),
            pl.BlockSpec((A_HEADS, A_KV_RANK, A_HEAD_DIM), lambda b, q: (0, 0, 0)),
            pl.BlockSpec((A_HEADS, N_PAT, Q_BLOCK, LANES), lambda b, q: (0, 0, 0, 0)),
        ],
        out_specs=pl.BlockSpec((1, Q_BLOCK, A_HEADS * A_HEAD_DIM), lambda b, q: (b, q, 0)),
        scratch_shapes=[
            pltpu.VMEM((wt, IDX_DIM), BF16),
            pltpu.VMEM((wt, A_KV_RANK), BF16),
            pltpu.VMEM((IDX_HEADS * Q_BLOCK, IDX_DIM), BF16),
            pltpu.VMEM((IDX_HEADS * Q_BLOCK, LANES), F32),
            pltpu.VMEM((Q_BLOCK, wt), I32),
            pltpu.VMEM((Q_BLOCK, wt), I32),
            pltpu.VMEM((Q_BLOCK, wt), F32),
            pltpu.VMEM((hq, A_KV_RANK), BF16),
            pltpu.VMEM((hq, wt), F32),
            pltpu.VMEM((hq, wt), BF16),
            pltpu.VMEM((hq, LANES), F32),
        ],
        compiler_params=_cparams("parallel", "arbitrary"),
        name=f"dsa_w{width}",
    )(proj3, proj3, proj3, proj3, proj3, ckv_meta, idx_meta, wuk, wuv, btab)


def _gla_kernel(gq_ref, gk_ref, gv_ref, go_ref, gr_ref, gkm_ref, gvm_ref, grm_ref, w2_ref, bg_ref,
                ng_ref, o_ref, st_scr, kk_scr, bb_scr):
    w2 = w2_ref[...].astype(BF16)

    def log_decay(gr):
        z = jnp.dot(gr.astype(BF16), w2, preferred_element_type=F32) + bg_ref[...]
        return (jnp.minimum(z, 0.0) - jnp.log(1.0 + jnp.exp(-jnp.abs(z)))) * (1.0 / G_GATE_NORM)

    def cumsum_rows(g):
        n = g.shape[0]
        tri = (lax.broadcasted_iota(I32, (n, n), 0) >= lax.broadcasted_iota(I32, (n, n), 1)).astype(F32)
        return jnp.dot(tri, g, precision=lax.Precision.HIGHEST, preferred_element_type=F32)

    @pl.when(pl.program_id(1) == 0)
    def _init_state():
        bm = cumsum_rows(log_decay(grm_ref[:, 0:G_GATE_RANK]))
        kd = gkm_ref[...] * jnp.exp(bm[N_META - 1:N_META, :] - bm)
        for h in range(G_HEADS):
            st_scr[h] = lax.dot_general(
                gvm_ref[:, h * G_VAL_DIM:(h + 1) * G_VAL_DIM].astype(BF16),
                kd[:, h * G_KEY_DIM:(h + 1) * G_KEY_DIM].astype(BF16), TN_DIMS,
                preferred_element_type=F32)

    b = cumsum_rows(log_decay(gr_ref[0][:, 0:G_GATE_RANK]))
    row = lax.broadcasted_iota(I32, (CHUNK, CHUNK), 0)
    col = lax.broadcasted_iota(I32, (CHUNK, CHUNK), 1)
    for h in range(G_HEADS):
        ks = slice(h * G_KEY_DIM, (h + 1) * G_KEY_DIM)
        vs = slice(h * G_VAL_DIM, (h + 1) * G_VAL_DIM)
        q = gq_ref[0][:, ks] * (G_KEY_DIM ** -0.5)
        k = gk_ref[0][:, ks]
        v = gv_ref[0][:, vs].astype(BF16)
        bh = b[:, ks]
        st = st_scr[h]
        o = lax.dot_general((q * jnp.exp(bh)).astype(BF16), st.astype(BF16), NT_DIMS,
                            preferred_element_type=F32)
        kk_scr[...] = k
        bb_scr[...] = bh

        def col_group(jg, a):
            j0 = pl.multiple_of(jg * SUBLANES, SUBLANES)
            k8 = kk_scr[pl.ds(j0, SUBLANES), :]
            b8 = bb_scr[pl.ds(j0, SUBLANES), :]
            for r in range(SUBLANES):
                w = q * k8[r:r + 1, :] * jnp.exp(jnp.minimum(bh - b8[r:r + 1, :], 0.0))
                a = jnp.where(col == j0 + r, jnp.sum(w, axis=1, keepdims=True), a)
            return a

        a = lax.fori_loop(0, CHUNK // SUBLANES, col_group, jnp.zeros((CHUNK, CHUNK), F32))
        a = jnp.where(row >= col, a, 0.0)
        o = o + jnp.dot(a.astype(BF16), v, preferred_element_type=F32)
        bl = bh[CHUNK - 1:CHUNK, :]
        kd = k * jnp.exp(bl - bh)
        st_scr[h] = st * jnp.exp(bl) + lax.dot_general(v, kd.astype(BF16), TN_DIMS,
                                                       preferred_element_type=F32)
        on = o * lax.rsqrt(jnp.mean(o * o, axis=1, keepdims=True) + EPS) * ng_ref[...]
        og = go_ref[0][:, vs]
        o_ref[0, :, vs] = on * (og * (1.0 / (1.0 + jnp.exp(-og))))


def _gla(proj3, proj_meta, w_gk2, b_gk, norm_g):
    bsz, s, _ = proj3.shape
    hk = G_HEADS * G_KEY_DIM
    hv = G_HEADS * G_VAL_DIM
    gkm = proj_meta[:, C_GK:C_GK + hk]
    gvm = proj_meta[:, C_GV:C_GV + hv]
    grm = proj_meta[:, C_GR:C_GR + LANES]
    full = lambda shp: pl.BlockSpec(shp, lambda b, c: (0,) * len(shp))
    return pl.pallas_call(
        _gla_kernel,
        out_shape=jax.ShapeDtypeStruct((bsz, s, hv), F32),
        grid=(bsz, s // CHUNK),
        in_specs=[
            pl.BlockSpec((1, CHUNK, hk), lambda b, c: (b, c, C_GQ // hk)),
            pl.BlockSpec((1, CHUNK, hk), lambda b, c: (b, c, C_GK // hk)),
            pl.BlockSpec((1, CHUNK, hv), lambda b, c: (b, c, C_GV // hv)),
            pl.BlockSpec((1, CHUNK, hv), lambda b, c: (b, c, C_GO // hv)),
            pl.BlockSpec((1, CHUNK, LANES), lambda b, c: (b, c, C_GR // LANES)),
            full((N_META, hk)), full((N_META, hv)), full((N_META, LANES)),
            full((G_GATE_RANK, hk)), full((1, hk)), full((1, G_VAL_DIM)),
        ],
        out_specs=pl.BlockSpec((1, CHUNK, hv), lambda b, c: (b, c, 0)),
        scratch_shapes=[pltpu.VMEM((G_HEADS, G_VAL_DIM, G_KEY_DIM), F32),
                        pltpu.VMEM((CHUNK, G_KEY_DIM), F32),
                        pltpu.VMEM((CHUNK, G_KEY_DIM), F32)],
        compiler_params=_cparams("parallel", "arbitrary"),
        name="gla",
    )(proj3, proj3, proj3, proj3, proj3, gkm, gvm, grm, w_gk2, b_gk, norm_g)


def _out_ln_kernel(ya_ref, yb_ref, h_ref, wa_ref, wb_ref, g_ref, b_ref, o_ref):
    mix = (jnp.dot(ya_ref[...].astype(BF16), wa_ref[...], preferred_element_type=F32)
           + jnp.dot(yb_ref[...].astype(BF16), wb_ref[...], preferred_element_type=F32))
    o_ref[...] = _layer_norm(ALPHA * h_ref[...] + mix, g_ref[...], b_ref[...])


def _out_ln(ya, yb, h, wa, wb, g, b, tm):
    m, d = h.shape
    ka, kb = ya.shape[1], yb.shape[1]
    return pl.pallas_call(
        _out_ln_kernel,
        out_shape=jax.ShapeDtypeStruct((m, d), F32),
        grid=(m // tm,),
        in_specs=[pl.BlockSpec((tm, ka), lambda i: (i, 0)),
                  pl.BlockSpec((tm, kb), lambda i: (i, 0)),
                  pl.BlockSpec((tm, d), lambda i: (i, 0)),
                  pl.BlockSpec((ka, d), lambda i: (0, 0)),
                  pl.BlockSpec((kb, d), lambda i: (0, 0)),
                  pl.BlockSpec((1, d), lambda i: (0, 0)),
                  pl.BlockSpec((1, d), lambda i: (0, 0))],
        out_specs=pl.BlockSpec((tm, d), lambda i: (i, 0)),
        compiler_params=_cparams("parallel"),
        name="out_ln",
    )(ya, yb, h, wa, wb, g, b)


def _top_rows(s, n_rows, k):
    rid = lax.broadcasted_iota(I32, (n_rows, LANES), 0)
    slot = lax.broadcasted_iota(I32, (k, LANES), 0)
    vals = jnp.zeros((k, LANES), F32)
    idxs = jnp.zeros((k, LANES), I32)
    for r in range(k):
        m = jnp.max(s, axis=0, keepdims=True)
        im = jnp.min(jnp.where(s == m, rid, n_rows), axis=0, keepdims=True)
        vals = jnp.where(slot == r, m, vals)
        idxs = jnp.where(slot == r, im, idxs)
        s = jnp.where(rid == im, -jnp.inf, s)
    return vals, idxs


def _route_kernel(h_ref, wq_ref, sk_ref, e_ref, g_ref, qt_scr, *, n_half):
    qt = lax.dot_general(wq_ref[...], h_ref[...].astype(BF16), NT_DIMS, preferred_element_type=F32)
    for half in range(n_half):
        qt_scr[half] = qt[:, half * LANES:(half + 1) * LANES].astype(BF16)
    n_cand = P_TOPK * P_TOPK
    cid = lax.broadcasted_iota(I32, (n_cand, LANES), 0)
    slot = lax.broadcasted_iota(I32, (P_TOPK, LANES), 0)
    half_dim = P_QDIM // 2

    def unit(u, carry):
        h = u // n_half
        half = u % n_half
        tops = []
        for c in range(2):
            d0 = pl.multiple_of((h * 2 + c) * half_dim, half_dim)
            s = jnp.dot(sk_ref[h, c], qt_scr[half, pl.ds(d0, half_dim), :], preferred_element_type=F32)
            tops.append(_top_rows(s, P_NKEYS, P_TOPK))
        (s1, i1), (s2, i2) = tops
        cand = jnp.concatenate([s1[a:a + 1, :] + s2 for a in range(P_TOPK)], axis=0)
        cidx = jnp.concatenate([i1[a:a + 1, :] * P_NKEYS + i2 for a in range(P_TOPK)], axis=0)
        top, pos = _top_rows(cand, n_cand, P_TOPK)
        eidx = jnp.zeros((P_TOPK, LANES), I32)
        for r in range(P_TOPK):
            er = jnp.max(jnp.where(cid == pos[r:r + 1, :], cidx, -1), axis=0, keepdims=True)
            eidx = jnp.where(slot == r, er, eidx)
        ex = jnp.exp(top - top[0:1, :])
        gate = ex / jnp.sum(ex, axis=0, keepdims=True)
        r0 = pl.multiple_of(h * P_TOPK, P_TOPK)
        e_ref[half, pl.ds(r0, P_TOPK), :] = eidx
        g_ref[half, pl.ds(r0, P_TOPK), :] = gate
        return carry

    lax.fori_loop(0, P_HEADS * n_half, unit, 0)


def _route(h1, wq_t, sk, tm):
    m, d = h1.shape
    n_half = tm // LANES
    nk = P_HEADS * P_TOPK
    kern = functools.partial(_route_kernel, n_half=n_half)
    return pl.pallas_call(
        kern,
        out_shape=(jax.ShapeDtypeStruct((m // LANES, nk, LANES), I32),
                   jax.ShapeDtypeStruct((m // LANES, nk, LANES), F32)),
        grid=(m // tm,),
        in_specs=[pl.BlockSpec((tm, d), lambda i: (i, 0)),
                  pl.BlockSpec(wq_t.shape, lambda i: (0, 0)),
                  pl.BlockSpec(sk.shape, lambda i: (0, 0, 0, 0))],
        out_specs=(pl.BlockSpec((n_half, nk, LANES), lambda i: (i, 0, 0)),
                   pl.BlockSpec((n_half, nk, LANES), lambda i: (i, 0, 0))),
        scratch_shapes=[pltpu.VMEM((n_half, wq_t.shape[0], LANES), BF16)],
        compiler_params=_cparams("parallel"),
        name="route",
    )(h1, wq_t, sk)


def _sublane_sums(rs):
    sub = lax.broadcasted_iota(I32, (SUBLANES, LANES), 0)
    lo4 = sub < 4
    lo2 = (sub % 4) < 2
    even = (sub % 2) == 0

    def lvl1(a, b):
        return jnp.where(lo4, a, b) + pltpu.roll(jnp.where(lo4, b, a), 4, 0)

    def lvl(a, b, mask, d):
        return jnp.where(mask, a + pltpu.roll(a, SUBLANES - d, 0), b + pltpu.roll(b, d, 0))

    c04, c26, c15, c37 = lvl1(rs[0], rs[4]), lvl1(rs[2], rs[6]), lvl1(rs[1], rs[5]), lvl1(rs[3], rs[7])
    return lvl(lvl(c04, c26, lo2, 2), lvl(c15, c37, lo2, 2), even, 1)


def _gelu_tanh(x):
    return 0.5 * x * (1.0 + jnp.tanh(math.sqrt(2.0 / math.pi) * (x + 0.044715 * (x * x * x))))


def _apply_kernel(idx_ref, idxn_ref, x_ref, gate_ref, g_ref, b_ref, uv_hbm, o_ref, *scratch,
                  tt, nk, dsub):
    buf = scratch[:tt]
    cb_scr, sem = scratch[tt:]
    i = pl.program_id(0)
    n = pl.num_programs(0)
    lane = lax.broadcasted_iota(I32, (nk, LANES), 1)
    lane0 = (i * tt) % LANES
    cnt = float(dsub * LANES)

    def issue(src_idx, t, slot):
        for k in range(nk):
            pltpu.make_async_copy(uv_hbm.at[src_idx[t, k]], buf[slot].at[k], sem.at[slot]).start()

    def wait(slot):
        pltpu.make_async_copy(uv_hbm.at[pl.ds(0, nk)], buf[slot], sem.at[slot]).wait()

    def compute(j):
        x = x_ref[j]
        gcol = jnp.sum(jnp.where(lane == lane0 + j, gate_ref[0], 0.0), axis=1, keepdims=True)
        zs = []
        for g in range(nk // SUBLANES):
            rs = []
            for k in range(SUBLANES):
                p = buf[j][g * SUBLANES + k, 0:dsub, :] * x
                r = p[0:SUBLANES]
                for c in range(1, dsub // SUBLANES):
                    r = r + p[c * SUBLANES:(c + 1) * SUBLANES]
                rs.append(r)
            zs.append(_sublane_sums(rs))
        act = jnp.sum(jnp.concatenate(zs, axis=0), axis=1, keepdims=True)
        cb_scr[...] = jnp.broadcast_to(_gelu_tanh(act) * gcol, (nk, LANES))
        accs = [None] * 4
        for k in range(nk):
            term = buf[j][k, dsub:2 * dsub, :] * cb_scr[k:k + 1, :]
            accs[k % 4] = term if accs[k % 4] is None else accs[k % 4] + term
        r = ALPHA * x + ((accs[0] + accs[1]) + (accs[2] + accs[3]))
        mu = jnp.sum(jnp.sum(r, axis=1, keepdims=True), axis=0, keepdims=True) / cnt
        rc = r - mu
        var = jnp.sum(jnp.sum(rc * rc, axis=1, keepdims=True), axis=0, keepdims=True) / cnt
        o_ref[j] = rc * lax.rsqrt(var + EPS) * g_ref[...] + b_ref[...]

    @pl.when(i == 0)
    def _prologue():
        for j in range(tt - 1):
            issue(idx_ref, j, j)

    def body(last):
        for j in range(tt):
            wait(j)
            if j == 0:
                issue(idx_ref, tt - 1, tt - 1)
            elif not last:
                issue(idxn_ref, j - 1, j - 1)
            compute(j)

    @pl.when(i + 1 < n)
    def _steady():
        body(False)

    @pl.when(i + 1 == n)
    def _final():
        body(True)


def _apply(idx, h1_3, gates, g3, b3, uv3, tt):
    m, nk = idx.shape
    dsub = h1_3.shape[1]
    n_steps = m // tt
    kern = functools.partial(_apply_kernel, tt=tt, nk=nk, dsub=dsub)
    return pl.pallas_call(
        kern,
        out_shape=jax.ShapeDtypeStruct(h1_3.shape, F32),
        grid=(n_steps,),
        in_specs=[pl.BlockSpec((tt, nk), lambda i: (i, 0), memory_space=pltpu.SMEM),
                  pl.BlockSpec((tt, nk), lambda i: (jnp.minimum(i + 1, n_steps - 1), 0),
                               memory_space=pltpu.SMEM),
                  pl.BlockSpec((tt, dsub, LANES), lambda i: (i, 0, 0)),
                  pl.BlockSpec((1, nk, LANES), lambda i: (i * tt // LANES, 0, 0)),
                  pl.BlockSpec((dsub, LANES), lambda i: (0, 0)),
                  pl.BlockSpec((dsub, LANES), lambda i: (0, 0)),
                  pl.BlockSpec(memory_space=pl.ANY)],
        out_specs=pl.BlockSpec((tt, dsub, LANES), lambda i: (i, 0, 0)),
        scratch_shapes=[pltpu.VMEM((nk, 2 * dsub, LANES), F32) for _ in range(tt)]
        + [pltpu.VMEM((nk, LANES), F32), pltpu.SemaphoreType.DMA((tt,))],
        compiler_params=_cparams("arbitrary"),
        name="peer_apply",
    )(idx, idx, h1_3, gates, g3, b3, uv3)


def _regroup_w_in(w):
    d = w.shape[0]
    widths = (A_HEADS * A_HEAD_DIM, A_KV_RANK, IDX_HEADS * IDX_DIM, IDX_DIM, IDX_HEADS,
              G_HEADS * G_KEY_DIM, G_HEADS * G_KEY_DIM, G_HEADS * G_VAL_DIM, G_GATE_RANK,
              G_HEADS * G_VAL_DIM)
    offs = np.concatenate([[0], np.cumsum(widths)])
    a_q, a_ckv, i_q, i_k, i_w, g_q, g_k, g_v, g_r, g_o = [w[:, offs[n]:offs[n + 1]] for n in range(10)]
    z = lambda n: jnp.zeros((d, n), w.dtype)
    cat = jnp.concatenate([a_q, i_q, g_v, g_o, g_q, g_k, a_ckv,
                           i_k, i_w, z(LANES - IDX_DIM - IDX_HEADS),
                           g_r, z(LANES - G_GATE_RANK)], axis=1)
    assert cat.shape[1] == N_COLS
    return cat.astype(BF16)


def _dsa_groups(n_qblocks, n_groups):
    per = -(-n_qblocks // n_groups)
    return [(lo, min(per, n_qblocks - lo)) for lo in range(0, n_qblocks, per)]


def kernel(x, meta_tokens, ln0_g, ln0_b, rel_bias, w_in, w_uk, w_uv, w_gk2, b_gk, gla_norm_g, w_out,
           ln1_g, ln1_b, w_pq, sub_keys, u_tab, v_tab, ln2_g, ln2_b):
    bsz, s, d = x.shape
    t = bsz * s
    topk = min(TOPK_MAX, s // 4)
    assert s % Q_BLOCK == 0 and d % (SUBLANES * LANES) == 0
    _check_bucket_saturation(s + N_META)
    row2 = lambda v: v.reshape(1, -1)

    wcat = _regroup_w_in(w_in[0])
    h, proj = _ln_proj(x.reshape(t, d), row2(ln0_g), row2(ln0_b), wcat, tm=512, tn=512)
    _, proj_meta = _ln_proj(meta_tokens, row2(ln0_g), row2(ln0_b), wcat, tm=N_META, tn=512)
    proj3 = proj.reshape(bsz, s, N_COLS)

    pad_rows = lambda a: jnp.pad(a, ((0, LANES - N_META), (0, 0)))
    ckv_meta = pad_rows(proj_meta[:, C_CKV:C_CKV + A_KV_RANK])
    idx_meta = pad_rows(proj_meta[:, C_IDX:C_IDX + LANES])
    btab = _bias_tab(rel_bias)
    wuk = w_uk[0].astype(BF16)
    wuv = w_uv[0].astype(BF16)
    ya_parts = []
    for q_lo, n_q in _dsa_groups(s // Q_BLOCK, 4):
        width = (q_lo + n_q) * Q_BLOCK
        ya_parts.append(_dsa(proj3, ckv_meta, idx_meta, wuk, wuv, btab, q_lo, n_q, width, topk))
    y_a = jnp.concatenate(ya_parts, axis=1).reshape(t, A_HEADS * A_HEAD_DIM)

    y_b = _gla(proj3, proj_meta, w_gk2[0], row2(b_gk[0]), row2(gla_norm_g[0])).reshape(t, -1)

    wo = w_out[0].astype(BF16)
    ka = A_HEADS * A_HEAD_DIM
    h1 = _out_ln(y_a, y_b, h, wo[:ka], wo[ka:], row2(ln1_g[0]), row2(ln1_b[0]), tm=256)

    wq_t = w_pq[0].T.astype(BF16)
    eidx, gates = _route(h1, wq_t, sub_keys[0].astype(BF16), tm=256)
    nk = P_HEADS * P_TOPK
    idx = eidx.transpose(0, 2, 1).reshape(t, nk)
    dsub = d // LANES
    uv3 = jnp.concatenate([u_tab[0].reshape(-1, dsub, LANES), v_tab[0].reshape(-1, dsub, LANES)], axis=1)
    out3 = _apply(idx, h1.reshape(t, dsub, LANES), gates, ln2_g[0].reshape(dsub, LANES),
                  ln2_b[0].reshape(dsub, LANES), uv3, tt=8)
    return out3.reshape(bsz, s, d)
```

```python
import functools
import math

import numpy as np
import jax
import jax.numpy as jnp
from jax import lax
from jax.experimental import pallas as pl
from jax.experimental.pallas import tpu as pltpu

F32 = jnp.float32
BF16 = jnp.bfloat16
I32 = jnp.int32

N_META = 16
CHUNK = 64
SUB_BLOCK = 16
EPS = 1e-5
DEPTH = 1
ALPHA = (2 * DEPTH) ** 0.25
A_HEADS = 8
A_HEAD_DIM = 128
A_KV_RANK = 256
IDX_HEADS = 16
IDX_DIM = 64
TOPK_MAX = 256
Q_BLOCK = 128
REL_BUCKETS = 32
REL_MAX_DIST = 128
G_HEADS = 4
G_VAL_DIM = 256
G_KEY_DIM = 128
G_GATE_RANK = 16
G_GATE_NORM = 16.0
P_HEADS = 8
P_NKEYS = 128
P_QDIM = 256
P_TOPK = 16

LANES = 128
SUBLANES = 8
VMEM_LIMIT = 56 * 1024 * 1024

C_AQ, C_IQ, C_GV, C_GO = 0, 1024, 2048, 3072
C_GQ, C_GK = 4096, 4608
C_CKV = 5120
C_IDX = 5376
C_GR = 5504
N_COLS = 5632

INT_MIN = -2 ** 31
NT_DIMS = (((1,), (1,)), ((), ()))
TN_DIMS = (((0,), (0,)), ((), ()))


def _cparams(*sem):
    return pltpu.CompilerParams(dimension_semantics=sem, vmem_limit_bytes=VMEM_LIMIT)


def _layer_norm(x, g, b):
    mu = jnp.mean(x, axis=-1, keepdims=True)
    xc = x - mu
    var = jnp.mean(xc * xc, axis=-1, keepdims=True)
    return xc * lax.rsqrt(var + EPS) * g + b


def _ln_proj_kernel(x_ref, g_ref, b_ref, w_ref, h_ref, p_ref, hb_ref):
    @pl.when(pl.program_id(1) == 0)
    def _():
        y = _layer_norm(x_ref[...], g_ref[...], b_ref[...])
        h_ref[...] = y
        hb_ref[...] = y.astype(BF16)

    p_ref[...] = jnp.dot(hb_ref[...], w_ref[...], preferred_element_type=F32)


def _ln_proj(x2, g, b, wcat, tm, tn):
    m, d = x2.shape
    n = wcat.shape[1]
    return pl.pallas_call(
        _ln_proj_kernel,
        out_shape=(jax.ShapeDtypeStruct((m, d), F32), jax.ShapeDtypeStruct((m, n), F32)),
        grid=(m // tm, n // tn),
        in_specs=[pl.BlockSpec((tm, d), lambda i, j: (i, 0)),
                  pl.BlockSpec((1, d), lambda i, j: (0, 0)),
                  pl.BlockSpec((1, d), lambda i, j: (0, 0)),
                  pl.BlockSpec((d, tn), lambda i, j: (0, j))],
        out_specs=(pl.BlockSpec((tm, d), lambda i, j: (i, 0)),
                   pl.BlockSpec((tm, tn), lambda i, j: (i, j))),
        scratch_shapes=[pltpu.VMEM((tm, d), BF16)],
        compiler_params=_cparams("parallel", "arbitrary"),
        name="ln_proj",
    )(x2, g, b, wcat)


N_PAT = 5


def _t5_bucket(rel):
    nb = REL_BUCKETS // 2
    max_exact = nb // 2
    base = jnp.where(rel > 0, nb, 0)
    n = jnp.abs(rel)
    nf = jnp.maximum(n, 1).astype(jnp.float32)
    large = max_exact + (jnp.log(nf / max_exact) / math.log(REL_MAX_DIST / max_exact)
                         * (nb - max_exact)).astype(jnp.int32)
    large = jnp.minimum(large, nb - 1)
    return base + jnp.where(n < max_exact, n, large)


def _bias_patterns():
    i = jnp.arange(Q_BLOCK, dtype=I32)[:, None]
    j = jnp.arange(LANES, dtype=I32)[None, :]
    rels = [j - i, j - i - LANES, j - i - 2 * LANES, j - N_META - i, j - N_META - i - LANES]
    return jnp.stack([_t5_bucket(r) for r in rels])


def _check_bucket_saturation(max_dist):
    nb = REL_BUCKETS // 2
    max_exact = nb // 2
    n = np.arange(LANES + 1, max_dist + 1, dtype=np.float64)
    large = max_exact + np.floor(np.log(n / max_exact) / math.log(REL_MAX_DIST / max_exact) * (nb - max_exact))
    assert np.all(large >= nb - 1 + 0.5), "relative-position buckets do not saturate within one key tile"


def _bias_tab_kernel(bk_ref, rb_ref, o_ref):
    h = pl.program_id(0)
    for p in range(N_PAT):
        bk = bk_ref[p]
        acc = jnp.zeros((Q_BLOCK, LANES), F32)
        for b in range(REL_BUCKETS):
            acc = jnp.where(bk == b, rb_ref[b, h], acc)
        o_ref[0, p] = acc


def _bias_tab(rel_bias):
    return pl.pallas_call(
        _bias_tab_kernel,
        out_shape=jax.ShapeDtypeStruct((A_HEADS, N_PAT, Q_BLOCK, LANES), F32),
        grid=(A_HEADS,),
        in_specs=[pl.BlockSpec((N_PAT, Q_BLOCK, LANES), lambda h: (0, 0, 0)),
                  pl.BlockSpec(memory_space=pltpu.SMEM)],
        out_specs=pl.BlockSpec((1, N_PAT, Q_BLOCK, LANES), lambda h: (h, 0, 0, 0)),
        compiler_params=_cparams("arbitrary"),
        name="bias_tab",
    )(_bias_patterns(), rel_bias)


def _dsa_kernel(aq_ref, iq_ref, idxq_ref, ckv_ref, idxk_ref, ckvm_ref, idxm_ref, wuk_ref, wuv_ref,
                bt_ref, o_ref,
                ik_scr, ckvb_scr, iqs_scr, l_scr, key_scr, el_scr, madd_scr, ql_scr, lg_scr, p_scr,
                den_scr, *, q_lo, width, topk):
    nt = 1 + width // LANES
    wt = nt * LANES
    qi = pl.program_id(1) + q_lo

    @pl.when(pl.program_id(1) == 0)
    def _stage_keys():
        ik_scr[0:LANES, :] = idxm_ref[:, 0:IDX_DIM].astype(BF16)
        ik_scr[LANES:wt, :] = idxk_ref[0, :, 0:IDX_DIM].astype(BF16)
        ckvb_scr[0:LANES, :] = ckvm_ref[...].astype(BF16)
        ckvb_scr[LANES:wt, :] = ckv_ref[0].astype(BF16)

    row = lax.broadcasted_iota(I32, (Q_BLOCK, LANES), 0)
    lane = lax.broadcasted_iota(I32, (Q_BLOCK, LANES), 1)
    kb = ((qi * Q_BLOCK + row) // CHUNK + 1) * CHUNK

    iq = iq_ref[0].astype(BF16)
    for h in range(IDX_HEADS):
        iqs_scr[h * Q_BLOCK:(h + 1) * Q_BLOCK, :] = iq[:, h * IDX_DIM:(h + 1) * IDX_DIM]
    iw = idxq_ref[0][:, IDX_DIM:IDX_DIM + IDX_HEADS] * (IDX_HEADS ** -0.5 * IDX_DIM ** -0.5)
    iw_b = [jnp.broadcast_to(iw[:, h:h + 1], (Q_BLOCK, LANES)) for h in range(IDX_HEADS)]
    for kt in range(nt):
        l_scr[...] = lax.dot_general(iqs_scr[...], ik_scr[kt * LANES:(kt + 1) * LANES, :], NT_DIMS,
                                     preferred_element_type=F32)
        acc = jnp.zeros((Q_BLOCK, LANES), F32)
        for h in range(IDX_HEADS):
            acc = acc + jnp.maximum(l_scr[h * Q_BLOCK:(h + 1) * Q_BLOCK, :], 0.0) * iw_b[h]
        adm = (lane < N_META) if kt == 0 else (lane + (kt - 1) * LANES < kb)
        bits = pltpu.bitcast(acc + 0.0, I32)
        key = jnp.where(bits < 0, bits ^ 0x7FFFFFFF, bits)
        key_scr[:, kt * LANES:(kt + 1) * LANES] = jnp.where(adm, key, INT_MIN)

    def count(pred):
        acc = jnp.zeros((Q_BLOCK, LANES), F32)
        for kt in range(nt):
            acc = acc + jnp.where(pred(kt), 1.0, 0.0)
        return jnp.sum(acc, axis=1, keepdims=True)

    def key_tile(kt):
        return key_scr[:, kt * LANES:(kt + 1) * LANES]

    def val_step(it, thr):
        cand = thr + jnp.left_shift(jnp.int32(1), 31 - it)
        cnt = count(lambda kt: key_tile(kt) >= cand)
        return jnp.where(cnt >= topk, cand, thr)

    thr = lax.fori_loop(0, 32, val_step, jnp.full((Q_BLOCK, 1), INT_MIN, I32))
    n_ge = count(lambda kt: key_tile(kt) >= thr)
    thr_adm = jnp.maximum(thr, INT_MIN + 1)
    for kt in range(nt):
        madd_scr[:, kt * LANES:(kt + 1) * LANES] = jnp.where(key_tile(kt) >= thr_adm, 0.0, -jnp.inf)
    tied = jnp.where((n_ge > topk) & (thr > INT_MIN), 1.0, 0.0)

    @pl.when(jnp.max(tied) > 0.0)
    def _break_ties():
        need = topk - count(lambda kt: key_tile(kt) > thr)
        big = jnp.int32(2 ** 30)
        for kt in range(nt):
            el_scr[:, kt * LANES:(kt + 1) * LANES] = jnp.where(key_tile(kt) == thr, lane + kt * LANES, big)

        def el_tile(kt):
            return el_scr[:, kt * LANES:(kt + 1) * LANES]

        nbits = max(1, int(wt - 1).bit_length())

        def pos_step(it, x):
            cand = x + jnp.left_shift(jnp.int32(1), nbits - 1 - it)
            cnt = count(lambda kt: el_tile(kt) < cand)
            return jnp.where(cnt < need, cand, x)

        xpos = lax.fori_loop(0, nbits, pos_step, jnp.zeros((Q_BLOCK, 1), I32))
        for kt in range(nt):
            sel = ((key_tile(kt) > thr) | (el_tile(kt) <= xpos)) & (key_tile(kt) > INT_MIN)
            madd_scr[:, kt * LANES:(kt + 1) * LANES] = jnp.where(sel, 0.0, -jnp.inf)

    aq = aq_ref[0].astype(BF16)
    for h in range(A_HEADS):
        qh = lax.dot_general(aq[:, h * A_HEAD_DIM:(h + 1) * A_HEAD_DIM], wuk_ref[h], NT_DIMS,
                             preferred_element_type=F32) * (A_HEAD_DIM ** -0.5)
        ql_scr[h * Q_BLOCK:(h + 1) * Q_BLOCK, :] = qh.astype(BF16)
    for kt in range(nt):
        lg_scr[:, kt * LANES:(kt + 1) * LANES] = lax.dot_general(
            ql_scr[...], ckvb_scr[kt * LANES:(kt + 1) * LANES, :], NT_DIMS, preferred_element_type=F32)

    def head_body(h, carry):
        r0 = pl.multiple_of(h * Q_BLOCK, Q_BLOCK)
        rows = pl.ds(r0, Q_BLOCK)
        m = jnp.full((Q_BLOCK, LANES), -jnp.inf, F32)
        for kt in range(nt):
            cols = slice(kt * LANES, (kt + 1) * LANES)
            pat = (3 + jnp.minimum(qi, 1)) if kt == 0 else jnp.clip(qi - (kt - 1), 0, 2)
            l = lg_scr[rows, cols] + bt_ref[h, pat] + madd_scr[:, cols]
            lg_scr[rows, cols] = l
            m = jnp.maximum(m, l)
        mrow = jnp.max(m, axis=1, keepdims=True)
        ssum = jnp.zeros((Q_BLOCK, LANES), F32)
        for kt in range(nt):
            cols = slice(kt * LANES, (kt + 1) * LANES)
            e = jnp.exp(lg_scr[rows, cols] - mrow)
            ssum = ssum + e
            p_scr[rows, cols] = e.astype(BF16)
        den_scr[rows, :] = jnp.broadcast_to(jnp.sum(ssum, axis=1, keepdims=True), (Q_BLOCK, LANES))
        return carry

    lax.fori_loop(0, A_HEADS, head_body, 0)
    o = jnp.dot(p_scr[...], ckvb_scr[...], preferred_element_type=F32)
    for h in range(A_HEADS):
        rs = slice(h * Q_BLOCK, (h + 1) * Q_BLOCK)
        oh = o[rs] / den_scr[rs, 0:1]
        o_ref[0, :, h * A_HEAD_DIM:(h + 1) * A_HEAD_DIM] = jnp.dot(
            oh.astype(BF16), wuv_ref[h], preferred_element_type=F32)


def _dsa(proj3, ckv_meta, idx_meta, wuk, wuv, btab, q_lo, n_q, width, topk):
    bsz, s, _ = proj3.shape
    nt = 1 + width // LANES
    wt = nt * LANES
    hq = A_HEADS * Q_BLOCK
    kern = functools.partial(_dsa_kernel, q_lo=q_lo, width=width, topk=topk)
    return pl.pallas_call(
        kern,
        out_shape=jax.ShapeDtypeStruct((bsz, n_q * Q_BLOCK, A_HEADS * A_HEAD_DIM), F32),
        grid=(bsz, n_q),
        in_specs=[
            pl.BlockSpec((1, Q_BLOCK, 1024), lambda b, q: (b, q + q_lo, C_AQ // 1024)),
            pl.BlockSpec((1, Q_BLOCK, 1024), lambda b, q: (b, q + q_lo, C_IQ // 1024)),
            pl.BlockSpec((1, Q_BLOCK, LANES), lambda b, q: (b, q + q_lo, C_IDX // LANES)),
            pl.BlockSpec((1, width, A_KV_RANK), lambda b, q: (b, 0, C_CKV // A_KV_RANK)),
            pl.BlockSpec((1, width, LANES), lambda b, q: (b, 0, C_IDX // LANES)),
            pl.BlockSpec((LANES, A_KV_RANK), lambda b, q: (0, 0)),
            pl.BlockSpec((LANES, LANES), lambda b, q: (0, 0)),
            pl.BlockSpec((A_HEADS, A_KV_RANK, A_HEAD_DIM), lambda b, q: (0, 0, 0)),
            pl.BlockSpec((A_HEADS, A_KV_RANK, A_HEAD_DIM), lambda b, q: (0, 0, 0)),
            pl.BlockSpec((A_HEADS, N_PAT, Q_BLOCK, LANES), lambda b, q: (0, 0, 0, 0)),
        ],
        out_specs=pl.BlockSpec((1, Q_BLOCK, A_HEADS * A_HEAD_DIM), lambda b, q: (b, q, 0)),
        scratch_shapes=[
            pltpu.VMEM((wt, IDX_DIM), BF16),
            pltpu.VMEM((wt, A_KV_RANK), BF16),
            pltpu.VMEM((IDX_HEADS * Q_BLOCK, IDX_DIM), BF16),
            pltpu.VMEM((IDX_HEADS * Q_BLOCK, LANES), F32),
            pltpu.VMEM((Q_BLOCK, wt), I32),
            pltpu.VMEM((Q_BLOCK, wt), I32),
            pltpu.VMEM((Q_BLOCK, wt), F32),
            pltpu.VMEM((hq, A_KV_RANK), BF16),
            pltpu.VMEM((hq, wt), F32),
            pltpu.VMEM((hq, wt), BF16),
            pltpu.VMEM((hq, LANES), F32),
        ],
        compiler_params=_cparams("parallel", "arbitrary"),
        name=f"dsa_w{width}",
    )(proj3, proj3, proj3, proj3, proj3, ckv_meta, idx_meta, wuk, wuv, btab)


def _gla_kernel(gq_ref, gk_ref, gv_ref, go_ref, gr_ref, gkm_ref, gvm_ref, grm_ref, w2_ref, bg_ref,
                ng_ref, o_ref, st_scr):
    w2 = w2_ref[...].astype(BF16)

    def log_decay(gr):
        z = jnp.dot(gr.astype(BF16), w2, preferred_element_type=F32) + bg_ref[...]
        return (jnp.minimum(z, 0.0) - jnp.log(1.0 + jnp.exp(-jnp.abs(z)))) * (1.0 / G_GATE_NORM)

    def cumsum_rows(g):
        n = g.shape[0]
        tri = (lax.broadcasted_iota(I32, (n, n), 0) >= lax.broadcasted_iota(I32, (n, n), 1)).astype(F32)
        return jnp.dot(tri, g, precision=lax.Precision.HIGHEST, preferred_element_type=F32)

    @pl.when(pl.program_id(1) == 0)
    def _init_state():
        bm = cumsum_rows(log_decay(grm_ref[:, 0:G_GATE_RANK]))
        kd = gkm_ref[...] * jnp.exp(bm[N_META - 1:N_META, :] - bm)
        for h in range(G_HEADS):
            st_scr[h] = lax.dot_general(
                gvm_ref[:, h * G_VAL_DIM:(h + 1) * G_VAL_DIM].astype(BF16),
                kd[:, h * G_KEY_DIM:(h + 1) * G_KEY_DIM].astype(BF16), TN_DIMS,
                preferred_element_type=F32)

    b = cumsum_rows(log_decay(gr_ref[0][:, 0:G_GATE_RANK]))
    row = lax.broadcasted_iota(I32, (CHUNK, CHUNK), 0)
    col = lax.broadcasted_iota(I32, (CHUNK, CHUNK), 1)
    col_sb = lax.broadcasted_iota(I32, (SUB_BLOCK, CHUNK), 1)
    for h in range(G_HEADS):
        ks = slice(h * G_KEY_DIM, (h + 1) * G_KEY_DIM)
        vs = slice(h * G_VAL_DIM, (h + 1) * G_VAL_DIM)
        q = gq_ref[0][:, ks] * (G_KEY_DIM ** -0.5)
        k = gk_ref[0][:, ks]
        v = gv_ref[0][:, vs].astype(BF16)
        bh = b[:, ks]
        st = st_scr[h]
        o = lax.dot_general((q * jnp.exp(bh)).astype(BF16), st.astype(BF16), NT_DIMS,
                            preferred_element_type=F32)
        blocks = []
        for ib in range(CHUNK // SUB_BLOCK):
            rs = slice(ib * SUB_BLOCK, (ib + 1) * SUB_BLOCK)
            qi, bi = q[rs], bh[rs]
            ai = jnp.zeros((SUB_BLOCK, CHUNK), F32)
            if ib > 0:
                br = bi[0:1, :]
                kt = k * jnp.exp(jnp.minimum(br - bh, 0.0))
                ai = lax.dot_general(qi * jnp.exp(bi - br), kt, NT_DIMS, precision=lax.Precision.HIGHEST,
                                     preferred_element_type=F32)
                ai = jnp.where(col_sb < ib * SUB_BLOCK, ai, 0.0)
            for j in range(ib * SUB_BLOCK, (ib + 1) * SUB_BLOCK):
                w = qi * k[j:j + 1, :] * jnp.exp(jnp.minimum(bi - bh[j:j + 1, :], 0.0))
                ai = jnp.where(col_sb == j, jnp.sum(w, axis=1, keepdims=True), ai)
            blocks.append(ai)
        a = jnp.concatenate(blocks, axis=0)
        a = jnp.where(row >= col, a, 0.0)
        o = o + jnp.dot(a.astype(BF16), v, preferred_element_type=F32)
        bl = bh[CHUNK - 1:CHUNK, :]
        kd = k * jnp.exp(bl - bh)
        st_scr[h] = st * jnp.exp(bl) + lax.dot_general(v, kd.astype(BF16), TN_DIMS,
                                                       preferred_element_type=F32)
        on = o * lax.rsqrt(jnp.mean(o * o, axis=1, keepdims=True) + EPS) * ng_ref[...]
        og = go_ref[0][:, vs]
        o_ref[0, :, vs] = on * (og * (1.0 / (1.0 + jnp.exp(-og))))


def _gla(proj3, proj_meta, w_gk2, b_gk, norm_g):
    bsz, s, _ = proj3.shape
    hk = G_HEADS * G_KEY_DIM
    hv = G_HEADS * G_VAL_DIM
    gkm = proj_meta[:, C_GK:C_GK + hk]
    gvm = proj_meta[:, C_GV:C_GV + hv]
    grm = proj_meta[:, C_GR:C_GR + LANES]
    full = lambda shp: pl.BlockSpec(shp, lambda b, c: (0,) * len(shp))
    return pl.pallas_call(
        _gla_kernel,
        out_shape=jax.ShapeDtypeStruct((bsz, s, hv), F32),
        grid=(bsz, s // CHUNK),
        in_specs=[
            pl.BlockSpec((1, CHUNK, hk), lambda b, c: (b, c, C_GQ // hk)),
            pl.BlockSpec((1, CHUNK, hk), lambda b, c: (b, c, C_GK // hk)),
            pl.BlockSpec((1, CHUNK, hv), lambda b, c: (b, c, C_GV // hv)),
            pl.BlockSpec((1, CHUNK, hv), lambda b, c: (b, c, C_GO // hv)),
            pl.BlockSpec((1, CHUNK, LANES), lambda b, c: (b, c, C_GR // LANES)),
            full((N_META, hk)), full((N_META, hv)), full((N_META, LANES)),
            full((G_GATE_RANK, hk)), full((1, hk)), full((1, G_VAL_DIM)),
        ],
        out_specs=pl.BlockSpec((1, CHUNK, hv), lambda b, c: (b, c, 0)),
        scratch_shapes=[pltpu.VMEM((G_HEADS, G_VAL_DIM, G_KEY_DIM), F32)],
        compiler_params=_cparams("parallel", "arbitrary"),
        name="gla",
    )(proj3, proj3, proj3, proj3, proj3, gkm, gvm, grm, w_gk2, b_gk, norm_g)


def _out_ln_kernel(ya_ref, yb_ref, h_ref, wa_ref, wb_ref, g_ref, b_ref, o_ref):
    mix = (jnp.dot(ya_ref[...].astype(BF16), wa_ref[...], preferred_element_type=F32)
           + jnp.dot(yb_ref[...].astype(BF16), wb_ref[...], preferred_element_type=F32))
    o_ref[...] = _layer_norm(ALPHA * h_ref[...] + mix, g_ref[...], b_ref[...])


def _out_ln(ya, yb, h, wa, wb, g, b, tm):
    m, d = h.shape
    ka, kb = ya.shape[1], yb.shape[1]
    return pl.pallas_call(
        _out_ln_kernel,
        out_shape=jax.ShapeDtypeStruct((m, d), F32),
        grid=(m // tm,),
        in_specs=[pl.BlockSpec((tm, ka), lambda i: (i, 0)),
                  pl.BlockSpec((tm, kb), lambda i: (i, 0)),
                  pl.BlockSpec((tm, d), lambda i: (i, 0)),
                  pl.BlockSpec((ka, d), lambda i: (0, 0)),
                  pl.BlockSpec((kb, d), lambda i: (0, 0)),
                  pl.BlockSpec((1, d), lambda i: (0, 0)),
                  pl.BlockSpec((1, d), lambda i: (0, 0))],
        out_specs=pl.BlockSpec((tm, d), lambda i: (i, 0)),
        compiler_params=_cparams("parallel"),
        name="out_ln",
    )(ya, yb, h, wa, wb, g, b)


RANK_NONE = 2 ** 20


def _top_rows(s, rank, k):
    slot = lax.broadcasted_iota(I32, (k, LANES), 0)
    vals = jnp.zeros((k, LANES), F32)
    idxs = jnp.zeros((k, LANES), I32)
    for r in range(k):
        m = jnp.max(s, axis=0, keepdims=True)
        im = jnp.min(jnp.where(s == m, rank, RANK_NONE), axis=0, keepdims=True)
        vals = jnp.where(slot == r, m, vals)
        idxs = jnp.where(slot == r, im, idxs)
        s = jnp.where(rank == im, -jnp.inf, s)
    return vals, idxs


def _product_candidates(s1, i1, s2, i2):
    sub = lax.broadcasted_iota(I32, (SUBLANES, LANES), 0)
    lo = sub < 4
    s2a, s2b = s2[0:SUBLANES], s2[SUBLANES:2 * SUBLANES]
    i2a, i2b = i2[0:SUBLANES], i2[SUBLANES:2 * SUBLANES]
    s2d = jnp.where(lo, s2a, pltpu.roll(s2a, 4, 0))
    i2d = jnp.where(lo, i2a, pltpu.roll(i2a, 4, 0))
    row = lambda v, a: jnp.broadcast_to(v[a:a + 1, :], (SUBLANES, LANES))
    pair = lambda v, a: jnp.where(lo, row(v, a), row(v, a + 1))
    e1 = i1 * P_NKEYS
    groups = [
        (row(s1, 0) + s2a, row(e1, 0) + i2a, sub, None),
        (row(s1, 0) + s2b, row(e1, 0) + i2b, sub + SUBLANES, None),
        (row(s1, 1) + s2a, row(e1, 1) + i2a, sub + P_TOPK, None),
        (row(s1, 2) + s2a, row(e1, 2) + i2a, sub + 2 * P_TOPK, sub < 5),
        (row(s1, 3) + s2a, row(e1, 3) + i2a, sub + 3 * P_TOPK, sub < 4),
        (pair(s1, 4) + s2d, pair(e1, 4) + i2d, jnp.where(lo, sub + 4 * P_TOPK, sub - 4 + 5 * P_TOPK),
         (sub < 3) | ((sub >= 4) & (sub < 6))),
        (pair(s1, 6) + s2d, pair(e1, 6) + i2d, jnp.where(lo, sub + 6 * P_TOPK, sub - 4 + 7 * P_TOPK),
         (sub < 2) | ((sub >= 4) & (sub < 6))),
        (s1[SUBLANES:] + row(s2, 0), e1[SUBLANES:] + row(i2, 0), (sub + SUBLANES) * P_TOPK, None),
    ]
    vals, eids, ranks = [], [], []
    for v, e, rk, ok in groups:
        vals.append(v if ok is None else jnp.where(ok, v, -jnp.inf))
        ranks.append(rk if ok is None else jnp.where(ok, rk, RANK_NONE))
        eids.append(e)
    cat = lambda xs: jnp.concatenate(xs, axis=0)
    return cat(vals), cat(ranks), cat(eids)


def _route_kernel(h_ref, wq_ref, sk_ref, e_ref, g_ref, qt_scr, *, n_half):
    qt = lax.dot_general(wq_ref[...], h_ref[...].astype(BF16), NT_DIMS, preferred_element_type=F32)
    for half in range(n_half):
        qt_scr[half] = qt[:, half * LANES:(half + 1) * LANES].astype(BF16)
    key_id = lax.broadcasted_iota(I32, (P_NKEYS, LANES), 0)
    slot = lax.broadcasted_iota(I32, (P_TOPK, LANES), 0)
    half_dim = P_QDIM // 2

    def unit(u, carry):
        h = u // n_half
        half = u % n_half
        tops = []
        for c in range(2):
            d0 = pl.multiple_of((h * 2 + c) * half_dim, half_dim)
            s = jnp.dot(sk_ref[h, c], qt_scr[half, pl.ds(d0, half_dim), :], preferred_element_type=F32)
            tops.append(_top_rows(s, key_id, P_TOPK))
        (s1, i1), (s2, i2) = tops
        cand, crank, cidx = _product_candidates(s1, i1, s2, i2)
        top, pos = _top_rows(cand, crank, P_TOPK)
        eidx = jnp.zeros((P_TOPK, LANES), I32)
        for r in range(P_TOPK):
            er = jnp.max(jnp.where(crank == pos[r:r + 1, :], cidx, -1), axis=0, keepdims=True)
            eidx = jnp.where(slot == r, er, eidx)
        ex = jnp.exp(top - top[0:1, :])
        gate = ex / jnp.sum(ex, axis=0, keepdims=True)
        r0 = pl.multiple_of(h * P_TOPK, P_TOPK)
        e_ref[half, pl.ds(r0, P_TOPK), :] = eidx
        g_ref[half, pl.ds(r0, P_TOPK), :] = gate
        return carry

    lax.fori_loop(0, P_HEADS * n_half, unit, 0)


def _route(h1, wq_t, sk, tm):
    m, d = h1.shape
    n_half = tm // LANES
    nk = P_HEADS * P_TOPK
    kern = functools.partial(_route_kernel, n_half=n_half)
    return pl.pallas_call(
        kern,
        out_shape=(jax.ShapeDtypeStruct((m // LANES, nk, LANES), I32),
                   jax.ShapeDtypeStruct((m // LANES, nk, LANES), F32)),
        grid=(m // tm,),
        in_specs=[pl.BlockSpec((tm, d), lambda i: (i, 0)),
                  pl.BlockSpec(wq_t.shape, lambda i: (0, 0)),
                  pl.BlockSpec(sk.shape, lambda i: (0, 0, 0, 0))],
        out_specs=(pl.BlockSpec((n_half, nk, LANES), lambda i: (i, 0, 0)),
                   pl.BlockSpec((n_half, nk, LANES), lambda i: (i, 0, 0))),
        scratch_shapes=[pltpu.VMEM((n_half, wq_t.shape[0], LANES), BF16)],
        compiler_params=_cparams("parallel"),
        name="route",
    )(h1, wq_t, sk)


def _sublane_sums(rs):
    sub = lax.broadcasted_iota(I32, (SUBLANES, LANES), 0)
    lo4 = sub < 4
    lo2 = (sub % 4) < 2
    even = (sub % 2) == 0

    def lvl1(a, b):
        return jnp.where(lo4, a, b) + pltpu.roll(jnp.where(lo4, b, a), 4, 0)

    def lvl(a, b, mask, d):
        return jnp.where(mask, a + pltpu.roll(a, SUBLANES - d, 0), b + pltpu.roll(b, d, 0))

    c04, c26, c15, c37 = lvl1(rs[0], rs[4]), lvl1(rs[2], rs[6]), lvl1(rs[1], rs[5]), lvl1(rs[3], rs[7])
    return lvl(lvl(c04, c26, lo2, 2), lvl(c15, c37, lo2, 2), even, 1)


def _gelu_tanh(x):
    return 0.5 * x * (1.0 + jnp.tanh(math.sqrt(2.0 / math.pi) * (x + 0.044715 * (x * x * x))))


def _apply_kernel(idx_ref, idxn_ref, x_ref, gate_ref, g_ref, b_ref, uv_hbm, o_ref, *scratch,
                  tt, nk, dsub):
    buf = scratch[:tt]
    cb_scr, sem = scratch[tt:]
    i = pl.program_id(0)
    n = pl.num_programs(0)
    lane = lax.broadcasted_iota(I32, (nk, LANES), 1)
    lane0 = (i * tt) % LANES
    cnt = float(dsub * LANES)

    def issue(src_idx, t, slot):
        for k in range(nk):
            pltpu.make_async_copy(uv_hbm.at[src_idx[t, k]], buf[slot].at[k], sem.at[slot]).start()

    def wait(slot):
        pltpu.make_async_copy(uv_hbm.at[pl.ds(0, nk)], buf[slot], sem.at[slot]).wait()

    def compute(j):
        x = x_ref[j]
        gcol = jnp.sum(jnp.where(lane == lane0 + j, gate_ref[0], 0.0), axis=1, keepdims=True)
        zs = []
        for g in range(nk // SUBLANES):
            rs = []
            for k in range(SUBLANES):
                p = buf[j][g * SUBLANES + k, 0:dsub, :] * x
                r = p[0:SUBLANES]
                for c in range(1, dsub // SUBLANES):
                    r = r + p[c * SUBLANES:(c + 1) * SUBLANES]
                rs.append(r)
            zs.append(_sublane_sums(rs))
        act = jnp.sum(jnp.concatenate(zs, axis=0), axis=1, keepdims=True)
        cb_scr[...] = jnp.broadcast_to(_gelu_tanh(act) * gcol, (nk, LANES))
        accs = [None] * 4
        for k in range(nk):
            term = buf[j][k, dsub:2 * dsub, :] * cb_scr[k:k + 1, :]
            accs[k % 4] = term if accs[k % 4] is None else accs[k % 4] + term
        r = ALPHA * x + ((accs[0] + accs[1]) + (accs[2] + accs[3]))
        mu = jnp.sum(jnp.sum(r, axis=1, keepdims=True), axis=0, keepdims=True) / cnt
        rc = r - mu
        var = jnp.sum(jnp.sum(rc * rc, axis=1, keepdims=True), axis=0, keepdims=True) / cnt
        o_ref[j] = rc * lax.rsqrt(var + EPS) * g_ref[...] + b_ref[...]

    @pl.when(i == 0)
    def _prologue():
        for j in range(tt - 1):
            issue(idx_ref, j, j)

    def body(last):
        for j in range(tt):
            wait(j)
            if j == 0:
                issue(idx_ref, tt - 1, tt - 1)
            elif not last:
                issue(idxn_ref, j - 1, j - 1)
            compute(j)

    @pl.when(i + 1 < n)
    def _steady():
        body(False)

    @pl.when(i + 1 == n)
    def _final():
        body(True)


def _apply(idx, h1_3, gates, g3, b3, uv3, tt):
    m, nk = idx.shape
    dsub = h1_3.shape[1]
    n_steps = m // tt
    kern = functools.partial(_apply_kernel, tt=tt, nk=nk, dsub=dsub)
    return pl.pallas_call(
        kern,
        out_shape=jax.ShapeDtypeStruct(h1_3.shape, F32),
        grid=(n_steps,),
        in_specs=[pl.BlockSpec((tt, nk), lambda i: (i, 0), memory_space=pltpu.SMEM),
                  pl.BlockSpec((tt, nk), lambda i: (jnp.minimum(i + 1, n_steps - 1), 0),
                               memory_space=pltpu.SMEM),
                  pl.BlockSpec((tt, dsub, LANES), lambda i: (i, 0, 0)),
                  pl.BlockSpec((1, nk, LANES), lambda i: (i * tt // LANES, 0, 0)),
                  pl.BlockSpec((dsub, LANES), lambda i: (0, 0)),
                  pl.BlockSpec((dsub, LANES), lambda i: (0, 0)),
                  pl.BlockSpec(memory_space=pl.ANY)],
        out_specs=pl.BlockSpec((tt, dsub, LANES), lambda i: (i, 0, 0)),
        scratch_shapes=[pltpu.VMEM((nk, 2 * dsub, LANES), F32) for _ in range(tt)]
        + [pltpu.VMEM((nk, LANES), F32), pltpu.SemaphoreType.DMA((tt,))],
        compiler_params=_cparams("arbitrary"),
        name="peer_apply",
    )(idx, idx, h1_3, gates, g3, b3, uv3)


def _regroup_w_in(w):
    d = w.shape[0]
    widths = (A_HEADS * A_HEAD_DIM, A_KV_RANK, IDX_HEADS * IDX_DIM, IDX_DIM, IDX_HEADS,
              G_HEADS * G_KEY_DIM, G_HEADS * G_KEY_DIM, G_HEADS * G_VAL_DIM, G_GATE_RANK,
              G_HEADS * G_VAL_DIM)
    offs = np.concatenate([[0], np.cumsum(widths)])
    a_q, a_ckv, i_q, i_k, i_w, g_q, g_k, g_v, g_r, g_o = [w[:, offs[n]:offs[n + 1]] for n in range(10)]
    z = lambda n: jnp.zeros((d, n), w.dtype)
    cat = jnp.concatenate([a_q, i_q, g_v, g_o, g_q, g_k, a_ckv,
                           i_k, i_w, z(LANES - IDX_DIM - IDX_HEADS),
                           g_r, z(LANES - G_GATE_RANK)], axis=1)
    assert cat.shape[1] == N_COLS
    return cat.astype(BF16)


def _dsa_groups(n_qblocks, n_groups):
    per = -(-n_qblocks // n_groups)
    return [(lo, min(per, n_qblocks - lo)) for lo in range(0, n_qblocks, per)]


def kernel(x, meta_tokens, ln0_g, ln0_b, rel_bias, w_in, w_uk, w_uv, w_gk2, b_gk, gla_norm_g, w_out,
           ln1_g, ln1_b, w_pq, sub_keys, u_tab, v_tab, ln2_g, ln2_b):
    bsz, s, d = x.shape
    t = bsz * s
    topk = min(TOPK_MAX, s // 4)
    assert s % Q_BLOCK == 0 and d % (SUBLANES * LANES) == 0
    _check_bucket_saturation(s + N_META)
    row2 = lambda v: v.reshape(1, -1)

    wcat = _regroup_w_in(w_in[0])
    h, proj = _ln_proj(x.reshape(t, d), row2(ln0_g), row2(ln0_b), wcat, tm=512, tn=512)
    _, proj_meta = _ln_proj(meta_tokens, row2(ln0_g), row2(ln0_b), wcat, tm=N_META, tn=512)
    proj3 = proj.reshape(bsz, s, N_COLS)

    pad_rows = lambda a: jnp.pad(a, ((0, LANES - N_META), (0, 0)))
    ckv_meta = pad_rows(proj_meta[:, C_CKV:C_CKV + A_KV_RANK])
    idx_meta = pad_rows(proj_meta[:, C_IDX:C_IDX + LANES])
    btab = _bias_tab(rel_bias)
    wuk = w_uk[0].astype(BF16)
    wuv = w_uv[0].astype(BF16)
    ya_parts = []
    for q_lo, n_q in _dsa_groups(s // Q_BLOCK, 4):
        width = (q_lo + n_q) * Q_BLOCK
        ya_parts.append(_dsa(proj3, ckv_meta, idx_meta, wuk, wuv, btab, q_lo, n_q, width, topk))
    y_a = jnp.concatenate(ya_parts, axis=1).reshape(t, A_HEADS * A_HEAD_DIM)

    y_b = _gla(proj3, proj_meta, w_gk2[0], row2(b_gk[0]), row2(gla_norm_g[0])).reshape(t, -1)

    wo = w_out[0].astype(BF16)
    ka = A_HEADS * A_HEAD_DIM
    h1 = _out_ln(y_a, y_b, h, wo[:ka], wo[ka:], row2(ln1_g[0]), row2(ln1_b[0]), tm=256)

    wq_t = w_pq[0].T.astype(BF16)
    eidx, gates = _route(h1, wq_t, sub_keys[0].astype(BF16), tm=256)
    nk = P_HEADS * P_TOPK
    idx = eidx.transpose(0, 2, 1).reshape(t, nk)
    dsub = d // LANES
    uv3 = jnp.concatenate([u_tab[0].reshape(-1, dsub, LANES), v_tab[0].reshape(-1, dsub, LANES)], axis=1)
    out3 = _apply(idx, h1.reshape(t, dsub, LANES), gates, ln2_g[0].reshape(dsub, LANES),
                  ln2_b[0].reshape(dsub, LANES), uv3, tt=8)
    return out3.reshape(bsz, s, d)
```

```python
import functools
import math

import numpy as np
import jax
import jax.numpy as jnp
from jax import lax
from jax.experimental import pallas as pl
from jax.experimental.pallas import tpu as pltpu

F32 = jnp.float32
BF16 = jnp.bfloat16
I32 = jnp.int32

N_META = 16
CHUNK = 64
SUB_BLOCK = 16
EPS = 1e-5
DEPTH = 1
ALPHA = (2 * DEPTH) ** 0.25
A_HEADS = 8
A_HEAD_DIM = 128
A_KV_RANK = 256
IDX_HEADS = 16
IDX_DIM = 64
TOPK_MAX = 256
Q_BLOCK = 128
REL_BUCKETS = 32
REL_MAX_DIST = 128
G_HEADS = 4
G_VAL_DIM = 256
G_KEY_DIM = 128
G_GATE_RANK = 16
G_GATE_NORM = 16.0
P_HEADS = 8
P_NKEYS = 128
P_QDIM = 256
P_TOPK = 16

LANES = 128
SUBLANES = 8
VMEM_LIMIT = 56 * 1024 * 1024

C_AQ, C_IQ, C_GV, C_GO = 0, 1024, 2048, 3072
C_GQ, C_GK = 4096, 4608
C_CKV = 5120
C_IDX = 5376
C_GR = 5504
N_COLS = 5632

INT_MIN = -2 ** 31
NT_DIMS = (((1,), (1,)), ((), ()))
TN_DIMS = (((0,), (0,)), ((), ()))


def _cparams(*sem):
    return pltpu.CompilerParams(dimension_semantics=sem, vmem_limit_bytes=VMEM_LIMIT)


def _layer_norm(x, g, b):
    mu = jnp.mean(x, axis=-1, keepdims=True)
    xc = x - mu
    var = jnp.mean(xc * xc, axis=-1, keepdims=True)
    return xc * lax.rsqrt(var + EPS) * g + b


def _ln_proj_kernel(x_ref, g_ref, b_ref, w_ref, h_ref, p_ref, hb_ref):
    @pl.when(pl.program_id(1) == 0)
    def _():
        y = _layer_norm(x_ref[...], g_ref[...], b_ref[...])
        h_ref[...] = y
        hb_ref[...] = y.astype(BF16)

    p_ref[...] = jnp.dot(hb_ref[...], w_ref[...], preferred_element_type=F32)


def _ln_proj(x2, g, b, wcat, tm, tn):
    m, d = x2.shape
    n = wcat.shape[1]
    return pl.pallas_call(
        _ln_proj_kernel,
        out_shape=(jax.ShapeDtypeStruct((m, d), F32), jax.ShapeDtypeStruct((m, n), F32)),
        grid=(m // tm, n // tn),
        in_specs=[pl.BlockSpec((tm, d), lambda i, j: (i, 0)),
                  pl.BlockSpec((1, d), lambda i, j: (0, 0)),
                  pl.BlockSpec((1, d), lambda i, j: (0, 0)),
                  pl.BlockSpec((d, tn), lambda i, j: (0, j))],
        out_specs=(pl.BlockSpec((tm, d), lambda i, j: (i, 0)),
                   pl.BlockSpec((tm, tn), lambda i, j: (i, j))),
        scratch_shapes=[pltpu.VMEM((tm, d), BF16)],
        compiler_params=_cparams("parallel", "arbitrary"),
        name="ln_proj",
    )(x2, g, b, wcat)


N_PAT = 5


def _t5_bucket(rel):
    nb = REL_BUCKETS // 2
    max_exact = nb // 2
    base = jnp.where(rel > 0, nb, 0)
    n = jnp.abs(rel)
    nf = jnp.maximum(n, 1).astype(jnp.float32)
    large = max_exact + (jnp.log(nf / max_exact) / math.log(REL_MAX_DIST / max_exact)
                         * (nb - max_exact)).astype(jnp.int32)
    large = jnp.minimum(large, nb - 1)
    return base + jnp.where(n < max_exact, n, large)


def _bias_patterns():
    i = jnp.arange(Q_BLOCK, dtype=I32)[:, None]
    j = jnp.arange(LANES, dtype=I32)[None, :]
    rels = [j - i, j - i - LANES, j - i - 2 * LANES, j - N_META - i, j - N_META - i - LANES]
    return jnp.stack([_t5_bucket(r) for r in rels])


def _check_bucket_saturation(max_dist):
    nb = REL_BUCKETS // 2
    max_exact = nb // 2
    n = np.arange(LANES + 1, max_dist + 1, dtype=np.float64)
    large = max_exact + np.floor(np.log(n / max_exact) / math.log(REL_MAX_DIST / max_exact) * (nb - max_exact))
    assert np.all(large >= nb - 1 + 0.5), "relative-position buckets do not saturate within one key tile"


def _bias_tab_kernel(bk_ref, rb_ref, o_ref):
    h = pl.program_id(0)
    for p in range(N_PAT):
        bk = bk_ref[p]
        acc = jnp.zeros((Q_BLOCK, LANES), F32)
        for b in range(REL_BUCKETS):
            acc = jnp.where(bk == b, rb_ref[b, h], acc)
        o_ref[0, p] = acc


def _bias_tab(rel_bias):
    return pl.pallas_call(
        _bias_tab_kernel,
        out_shape=jax.ShapeDtypeStruct((A_HEADS, N_PAT, Q_BLOCK, LANES), F32),
        grid=(A_HEADS,),
        in_specs=[pl.BlockSpec((N_PAT, Q_BLOCK, LANES), lambda h: (0, 0, 0)),
                  pl.BlockSpec(memory_space=pltpu.SMEM)],
        out_specs=pl.BlockSpec((1, N_PAT, Q_BLOCK, LANES), lambda h: (h, 0, 0, 0)),
        compiler_params=_cparams("arbitrary"),
        name="bias_tab",
    )(_bias_patterns(), rel_bias)


def _dsa_kernel(aq_ref, iq_ref, idxq_ref, ckv_ref, idxk_ref, ckvm_ref, idxm_ref, wuk_ref, wuv_ref,
                bt_ref, o_ref,
                ik_scr, ckvb_scr, iqs_scr, l_scr, key_scr, el_scr, madd_scr, ql_scr, lg_scr, p_scr,
                den_scr, *, q_lo, width, topk):
    nt = 1 + width // LANES
    wt = nt * LANES
    qi = pl.program_id(1) + q_lo

    @pl.when(pl.program_id(1) == 0)
    def _stage_keys():
        ik_scr[0:LANES, :] = idxm_ref[:, 0:IDX_DIM].astype(BF16)
        ik_scr[LANES:wt, :] = idxk_ref[0, :, 0:IDX_DIM].astype(BF16)
        ckvb_scr[0:LANES, :] = ckvm_ref[...].astype(BF16)
        ckvb_scr[LANES:wt, :] = ckv_ref[0].astype(BF16)

    row = lax.broadcasted_iota(I32, (Q_BLOCK, LANES), 0)
    lane = lax.broadcasted_iota(I32, (Q_BLOCK, LANES), 1)
    kb = ((qi * Q_BLOCK + row) // CHUNK + 1) * CHUNK

    iq = iq_ref[0].astype(BF16)
    for h in range(IDX_HEADS):
        iqs_scr[h * Q_BLOCK:(h + 1) * Q_BLOCK, :] = iq[:, h * IDX_DIM:(h + 1) * IDX_DIM]
    iw = idxq_ref[0][:, IDX_DIM:IDX_DIM + IDX_HEADS] * (IDX_HEADS ** -0.5 * IDX_DIM ** -0.5)
    iw_b = [jnp.broadcast_to(iw[:, h:h + 1], (Q_BLOCK, LANES)) for h in range(IDX_HEADS)]
    for kt in range(nt):
        l_scr[...] = lax.dot_general(iqs_scr[...], ik_scr[kt * LANES:(kt + 1) * LANES, :], NT_DIMS,
                                     preferred_element_type=F32)
        acc = jnp.zeros((Q_BLOCK, LANES), F32)
        for h in range(IDX_HEADS):
            acc = acc + jnp.maximum(l_scr[h * Q_BLOCK:(h + 1) * Q_BLOCK, :], 0.0) * iw_b[h]
        adm = (lane < N_META) if kt == 0 else (lane + (kt - 1) * LANES < kb)
        bits = pltpu.bitcast(acc + 0.0, I32)
        key = jnp.where(bits < 0, bits ^ 0x7FFFFFFF, bits)
        key_scr[:, kt * LANES:(kt + 1) * LANES] = jnp.where(adm, key, INT_MIN)

    def count(pred):
        acc = jnp.zeros((Q_BLOCK, LANES), F32)
        for kt in range(nt):
            acc = acc + jnp.where(pred(kt), 1.0, 0.0)
        return jnp.sum(acc, axis=1, keepdims=True)

    def key_tile(kt):
        return key_scr[:, kt * LANES:(kt + 1) * LANES]

    def val_step(it, thr):
        cand = thr + jnp.left_shift(jnp.int32(1), 31 - it)
        cnt = count(lambda kt: key_tile(kt) >= cand)
        return jnp.where(cnt >= topk, cand, thr)

    thr = lax.fori_loop(0, 32, val_step, jnp.full((Q_BLOCK, 1), INT_MIN, I32))
    n_ge = count(lambda kt: key_tile(kt) >= thr)
    thr_adm = jnp.maximum(thr, INT_MIN + 1)
    for kt in range(nt):
        madd_scr[:, kt * LANES:(kt + 1) * LANES] = jnp.where(key_tile(kt) >= thr_adm, 0.0, -jnp.inf)
    tied = jnp.where((n_ge > topk) & (thr > INT_MIN), 1.0, 0.0)

    @pl.when(jnp.max(tied) > 0.0)
    def _break_ties():
        need = topk - count(lambda kt: key_tile(kt) > thr)
        big = jnp.int32(2 ** 30)
        for kt in range(nt):
            el_scr[:, kt * LANES:(kt + 1) * LANES] = jnp.where(key_tile(kt) == thr, lane + kt * LANES, big)

        def el_tile(kt):
            return el_scr[:, kt * LANES:(kt + 1) * LANES]

        nbits = max(1, int(wt - 1).bit_length())

        def pos_step(it, x):
            cand = x + jnp.left_shift(jnp.int32(1), nbits - 1 - it)
            cnt = count(lambda kt: el_tile(kt) < cand)
            return jnp.where(cnt < need, cand, x)

        xpos = lax.fori_loop(0, nbits, pos_step, jnp.zeros((Q_BLOCK, 1), I32))
        for kt in range(nt):
            sel = ((key_tile(kt) > thr) | (el_tile(kt) <= xpos)) & (key_tile(kt) > INT_MIN)
            madd_scr[:, kt * LANES:(kt + 1) * LANES] = jnp.where(sel, 0.0, -jnp.inf)

    aq = aq_ref[0].astype(BF16)
    for h in range(A_HEADS):
        qh = lax.dot_general(aq[:, h * A_HEAD_DIM:(h + 1) * A_HEAD_DIM], wuk_ref[h], NT_DIMS,
                             preferred_element_type=F32) * (A_HEAD_DIM ** -0.5)
        ql_scr[h * Q_BLOCK:(h + 1) * Q_BLOCK, :] = qh.astype(BF16)
    for kt in range(nt):
        lg_scr[:, kt * LANES:(kt + 1) * LANES] = lax.dot_general(
            ql_scr[...], ckvb_scr[kt * LANES:(kt + 1) * LANES, :], NT_DIMS, preferred_element_type=F32)

    def head_body(h, carry):
        r0 = pl.multiple_of(h * Q_BLOCK, Q_BLOCK)
        rows = pl.ds(r0, Q_BLOCK)
        m = jnp.full((Q_BLOCK, LANES), -jnp.inf, F32)
        for kt in range(nt):
            cols = slice(kt * LANES, (kt + 1) * LANES)
            pat = (3 + jnp.minimum(qi, 1)) if kt == 0 else jnp.clip(qi - (kt - 1), 0, 2)
            l = lg_scr[rows, cols] + bt_ref[h, pat] + madd_scr[:, cols]
            lg_scr[rows, cols] = l
            m = jnp.maximum(m, l)
        mrow = jnp.max(m, axis=1, keepdims=True)
        ssum = jnp.zeros((Q_BLOCK, LANES), F32)
        for kt in range(nt):
            cols = slice(kt * LANES, (kt + 1) * LANES)
            e = jnp.exp(lg_scr[rows, cols] - mrow)
            ssum = ssum + e
            p_scr[rows, cols] = e.astype(BF16)
        den_scr[rows, :] = jnp.broadcast_to(jnp.sum(ssum, axis=1, keepdims=True), (Q_BLOCK, LANES))
        return carry

    lax.fori_loop(0, A_HEADS, head_body, 0)
    o = jnp.dot(p_scr[...], ckvb_scr[...], preferred_element_type=F32)
    for h in range(A_HEADS):
        rs = slice(h * Q_BLOCK, (h + 1) * Q_BLOCK)
        oh = o[rs] / den_scr[rs, 0:1]
        o_ref[0, :, h * A_HEAD_DIM:(h + 1) * A_HEAD_DIM] = jnp.dot(
            oh.astype(BF16), wuv_ref[h], preferred_element_type=F32)


def _dsa(proj3, ckv_meta, idx_meta, wuk, wuv, btab, q_lo, n_q, width, topk):
    bsz, s, _ = proj3.shape
    nt = 1 + width // LANES
    wt = nt * LANES
    hq = A_HEADS * Q_BLOCK
    kern = functools.partial(_dsa_kernel, q_lo=q_lo, width=width, topk=topk)
    return pl.pallas_call(
        kern,
        out_shape=jax.ShapeDtypeStruct((bsz, n_q * Q_BLOCK, A_HEADS * A_HEAD_DIM), F32),
        grid=(bsz, n_q),
        in_specs=[
            pl.BlockSpec((1, Q_BLOCK, 1024), lambda b, q: (b, q + q_lo, C_AQ // 1024)),
            pl.BlockSpec((1, Q_BLOCK, 1024), lambda b, q: (b, q + q_lo, C_IQ // 1024)),
            pl.BlockSpec((1, Q_BLOCK, LANES), lambda b, q: (b, q + q_lo, C_IDX // LANES)),
            pl.BlockSpec((1, width, A_KV_RANK), lambda b, q: (b, 0, C_CKV // A_KV_RANK)),
            pl.BlockSpec((1, width, LANES), lambda b, q: (b, 0, C_IDX // LANES)),
            pl.BlockSpec((LANES, A_KV_RANK), lambda b, q: (0, 0)),
            pl.BlockSpec((LANES, LANES), lambda b, q: (0, 0)),
            pl.BlockSpec((A_HEADS, A_KV_RANK, A_HEAD_DIM), lambda b, q: (0, 0, 0)),
            pl.BlockSpec((A_HEADS, A_KV_RANK, A_HEAD_DIM), lambda b, q: (0, 0, 0)),
            pl.BlockSpec((A_HEADS, N_PAT, Q_BLOCK, LANES), lambda b, q: (0, 0, 0, 0)),
        ],
        out_specs=pl.BlockSpec((1, Q_BLOCK, A_HEADS * A_HEAD_DIM), lambda b, q: (b, q, 0)),
        scratch_shapes=[
            pltpu.VMEM((wt, IDX_DIM), BF16),
            pltpu.VMEM((wt, A_KV_RANK), BF16),
            pltpu.VMEM((IDX_HEADS * Q_BLOCK, IDX_DIM), BF16),
            pltpu.VMEM((IDX_HEADS * Q_BLOCK, LANES), F32),
            pltpu.VMEM((Q_BLOCK, wt), I32),
            pltpu.VMEM((Q_BLOCK, wt), I32),
            pltpu.VMEM((Q_BLOCK, wt), F32),
            pltpu.VMEM((hq, A_KV_RANK), BF16),
            pltpu.VMEM((hq, wt), F32),
            pltpu.VMEM((hq, wt), BF16),
            pltpu.VMEM((hq, LANES), F32),
        ],
        compiler_params=_cparams("parallel", "arbitrary"),
        name=f"dsa_w{width}",
    )(proj3, proj3, proj3, proj3, proj3, ckv_meta, idx_meta, wuk, wuv, btab)


def _gla_kernel(gq_ref, gk_ref, gv_ref, go_ref, gr_ref, gkm_ref, gvm_ref, grm_ref, w2_ref, bg_ref,
                ng_ref, o_ref, st_scr):
    w2 = w2_ref[...].astype(BF16)

    def log_decay(gr):
        z = jnp.dot(gr.astype(BF16), w2, preferred_element_type=F32) + bg_ref[...]
        return (jnp.minimum(z, 0.0) - jnp.log(1.0 + jnp.exp(-jnp.abs(z)))) * (1.0 / G_GATE_NORM)

    def cumsum_rows(g):
        n = g.shape[0]
        tri = (lax.broadcasted_iota(I32, (n, n), 0) >= lax.broadcasted_iota(I32, (n, n), 1)).astype(F32)
        return jnp.dot(tri, g, precision=lax.Precision.HIGHEST, preferred_element_type=F32)

    @pl.when(pl.program_id(1) == 0)
    def _init_state():
        bm = cumsum_rows(log_decay(grm_ref[:, 0:G_GATE_RANK]))
        kd = gkm_ref[...] * jnp.exp(bm[N_META - 1:N_META, :] - bm)
        for h in range(G_HEADS):
            st_scr[h] = lax.dot_general(
                gvm_ref[:, h * G_VAL_DIM:(h + 1) * G_VAL_DIM].astype(BF16),
                kd[:, h * G_KEY_DIM:(h + 1) * G_KEY_DIM].astype(BF16), TN_DIMS,
                preferred_element_type=F32)

    b = cumsum_rows(log_decay(gr_ref[0][:, 0:G_GATE_RANK]))
    row = lax.broadcasted_iota(I32, (CHUNK, CHUNK), 0)
    col = lax.broadcasted_iota(I32, (CHUNK, CHUNK), 1)
    col_sb = lax.broadcasted_iota(I32, (SUB_BLOCK, CHUNK), 1)
    for h in range(G_HEADS):
        ks = slice(h * G_KEY_DIM, (h + 1) * G_KEY_DIM)
        vs = slice(h * G_VAL_DIM, (h + 1) * G_VAL_DIM)
        q = gq_ref[0][:, ks] * (G_KEY_DIM ** -0.5)
        k = gk_ref[0][:, ks]
        v = gv_ref[0][:, vs].astype(BF16)
        bh = b[:, ks]
        st = st_scr[h]
        o = lax.dot_general((q * jnp.exp(bh)).astype(BF16), st.astype(BF16), NT_DIMS,
                            preferred_element_type=F32)
        blocks = []
        for ib in range(CHUNK // SUB_BLOCK):
            rs = slice(ib * SUB_BLOCK, (ib + 1) * SUB_BLOCK)
            qi, bi = q[rs], bh[rs]
            ai = jnp.zeros((SUB_BLOCK, CHUNK), F32)
            if ib > 0:
                br = bi[0:1, :]
                kt = k * jnp.exp(jnp.minimum(br - bh, 0.0))
                ai = lax.dot_general(qi * jnp.exp(bi - br), kt, NT_DIMS, precision=lax.Precision.HIGHEST,
                                     preferred_element_type=F32)
                ai = jnp.where(col_sb < ib * SUB_BLOCK, ai, 0.0)
            for j in range(ib * SUB_BLOCK, (ib + 1) * SUB_BLOCK):
                w = qi * k[j:j + 1, :] * jnp.exp(jnp.minimum(bi - bh[j:j + 1, :], 0.0))
                ai = jnp.where(col_sb == j, jnp.sum(w, axis=1, keepdims=True), ai)
            blocks.append(ai)
        a = jnp.concatenate(blocks, axis=0)
        a = jnp.where(row >= col, a, 0.0)
        o = o + jnp.dot(a.astype(BF16), v, preferred_element_type=F32)
        bl = bh[CHUNK - 1:CHUNK, :]
        kd = k * jnp.exp(bl - bh)
        st_scr[h] = st * jnp.exp(bl) + lax.dot_general(v, kd.astype(BF16), TN_DIMS,
                                                       preferred_element_type=F32)
        on = o * lax.rsqrt(jnp.mean(o * o, axis=1, keepdims=True) + EPS) * ng_ref[...]
        og = go_ref[0][:, vs]
        o_ref[0, :, vs] = on * (og * (1.0 / (1.0 + jnp.exp(-og))))


def _gla(proj3, proj_meta, w_gk2, b_gk, norm_g):
    bsz, s, _ = proj3.shape
    hk = G_HEADS * G_KEY_DIM
    hv = G_HEADS * G_VAL_DIM
    gkm = proj_meta[:, C_GK:C_GK + hk]
    gvm = proj_meta[:, C_GV:C_GV + hv]
    grm = proj_meta[:, C_GR:C_GR + LANES]
    full = lambda shp: pl.BlockSpec(shp, lambda b, c: (0,) * len(shp))
    return pl.pallas_call(
        _gla_kernel,
        out_shape=jax.ShapeDtypeStruct((bsz, s, hv), F32),
        grid=(bsz, s // CHUNK),
        in_specs=[
            pl.BlockSpec((1, CHUNK, hk), lambda b, c: (b, c, C_GQ // hk)),
            pl.BlockSpec((1, CHUNK, hk), lambda b, c: (b, c, C_GK // hk)),
            pl.BlockSpec((1, CHUNK, hv), lambda b, c: (b, c, C_GV // hv)),
            pl.BlockSpec((1, CHUNK, hv), lambda b, c: (b, c, C_GO // hv)),
            pl.BlockSpec((1, CHUNK, LANES), lambda b, c: (b, c, C_GR // LANES)),
            full((N_META, hk)), full((N_META, hv)), full((N_META, LANES)),
            full((G_GATE_RANK, hk)), full((1, hk)), full((1, G_VAL_DIM)),
        ],
        out_specs=pl.BlockSpec((1, CHUNK, hv), lambda b, c: (b, c, 0)),
        scratch_shapes=[pltpu.VMEM((G_HEADS, G_VAL_DIM, G_KEY_DIM), F32)],
        compiler_params=_cparams("parallel", "arbitrary"),
        name="gla",
    )(proj3, proj3, proj3, proj3, proj3, gkm, gvm, grm, w_gk2, b_gk, norm_g)


def _out_ln_kernel(ya_ref, yb_ref, h_ref, wa_ref, wb_ref, g_ref, b_ref, o_ref):
    mix = (jnp.dot(ya_ref[...].astype(BF16), wa_ref[...], preferred_element_type=F32)
           + jnp.dot(yb_ref[...].astype(BF16), wb_ref[...], preferred_element_type=F32))
    o_ref[...] = _layer_norm(ALPHA * h_ref[...] + mix, g_ref[...], b_ref[...])


def _out_ln(ya, yb, h, wa, wb, g, b, tm):
    m, d = h.shape
    ka, kb = ya.shape[1], yb.shape[1]
    return pl.pallas_call(
        _out_ln_kernel,
        out_shape=jax.ShapeDtypeStruct((m, d), F32),
        grid=(m // tm,),
        in_specs=[pl.BlockSpec((tm, ka), lambda i: (i, 0)),
                  pl.BlockSpec((tm, kb), lambda i: (i, 0)),
                  pl.BlockSpec((tm, d), lambda i: (i, 0)),
                  pl.BlockSpec((ka, d), lambda i: (0, 0)),
                  pl.BlockSpec((kb, d), lambda i: (0, 0)),
                  pl.BlockSpec((1, d), lambda i: (0, 0)),
                  pl.BlockSpec((1, d), lambda i: (0, 0))],
        out_specs=pl.BlockSpec((tm, d), lambda i: (i, 0)),
        compiler_params=_cparams("parallel"),
        name="out_ln",
    )(ya, yb, h, wa, wb, g, b)


RANK_NONE = 2 ** 20


def _top_rows(s, rank, k):
    slot = lax.broadcasted_iota(I32, (k, LANES), 0)
    vals = jnp.zeros((k, LANES), F32)
    idxs = jnp.zeros((k, LANES), I32)
    for r in range(k):
        m = jnp.max(s, axis=0, keepdims=True)
        im = jnp.min(jnp.where(s == m, rank, RANK_NONE), axis=0, keepdims=True)
        vals = jnp.where(slot == r, m, vals)
        idxs = jnp.where(slot == r, im, idxs)
        s = jnp.where(rank == im, -jnp.inf, s)
    return vals, idxs


def _product_candidates(s1, i1, s2, i2):
    sub = lax.broadcasted_iota(I32, (SUBLANES, LANES), 0)
    lo = sub < 4
    s2a, s2b = s2[0:SUBLANES], s2[SUBLANES:2 * SUBLANES]
    i2a, i2b = i2[0:SUBLANES], i2[SUBLANES:2 * SUBLANES]
    s2d = jnp.where(lo, s2a, pltpu.roll(s2a, 4, 0))
    i2d = jnp.where(lo, i2a, pltpu.roll(i2a, 4, 0))
    row = lambda v, a: jnp.broadcast_to(v[a:a + 1, :], (SUBLANES, LANES))
    pair = lambda v, a: jnp.where(lo, row(v, a), row(v, a + 1))
    e1 = i1 * P_NKEYS
    groups = [
        (row(s1, 0) + s2a, row(e1, 0) + i2a, sub, None),
        (row(s1, 0) + s2b, row(e1, 0) + i2b, sub + SUBLANES, None),
        (row(s1, 1) + s2a, row(e1, 1) + i2a, sub + P_TOPK, None),
        (row(s1, 2) + s2a, row(e1, 2) + i2a, sub + 2 * P_TOPK, sub < 5),
        (row(s1, 3) + s2a, row(e1, 3) + i2a, sub + 3 * P_TOPK, sub < 4),
        (pair(s1, 4) + s2d, pair(e1, 4) + i2d, jnp.where(lo, sub + 4 * P_TOPK, sub - 4 + 5 * P_TOPK),
         (sub < 3) | ((sub >= 4) & (sub < 6))),
        (pair(s1, 6) + s2d, pair(e1, 6) + i2d, jnp.where(lo, sub + 6 * P_TOPK, sub - 4 + 7 * P_TOPK),
         (sub < 2) | ((sub >= 4) & (sub < 6))),
        (s1[SUBLANES:] + row(s2, 0), e1[SUBLANES:] + row(i2, 0), (sub + SUBLANES) * P_TOPK, None),
    ]
    vals, eids, ranks = [], [], []
    for v, e, rk, ok in groups:
        vals.append(v if ok is None else jnp.where(ok, v, -jnp.inf))
        ranks.append(rk if ok is None else jnp.where(ok, rk, RANK_NONE))
        eids.append(e)
    cat = lambda xs: jnp.concatenate(xs, axis=0)
    return cat(vals), cat(ranks), cat(eids)


def _route_kernel(h_ref, wq_ref, sk_ref, e_ref, g_ref, qt_scr, *, n_half):
    qt = lax.dot_general(wq_ref[...], h_ref[...].astype(BF16), NT_DIMS, preferred_element_type=F32)
    for half in range(n_half):
        qt_scr[half] = qt[:, half * LANES:(half + 1) * LANES].astype(BF16)
    key_id = lax.broadcasted_iota(I32, (P_NKEYS, LANES), 0)
    slot = lax.broadcasted_iota(I32, (P_TOPK, LANES), 0)
    half_dim = P_QDIM // 2

    def unit(u, carry):
        h = u // n_half
        half = u % n_half
        tops = []
        for c in range(2):
            d0 = pl.multiple_of((h * 2 + c) * half_dim, half_dim)
            s = jnp.dot(sk_ref[h, c], qt_scr[half, pl.ds(d0, half_dim), :], preferred_element_type=F32)
            tops.append(_top_rows(s, key_id, P_TOPK))
        (s1, i1), (s2, i2) = tops
        cand, crank, cidx = _product_candidates(s1, i1, s2, i2)
        top, pos = _top_rows(cand, crank, P_TOPK)
        eidx = jnp.zeros((P_TOPK, LANES), I32)
        for r in range(P_TOPK):
            er = jnp.max(jnp.where(crank == pos[r:r + 1, :], cidx, -1), axis=0, keepdims=True)
            eidx = jnp.where(slot == r, er, eidx)
        ex = jnp.exp(top - top[0:1, :])
        gate = ex / jnp.sum(ex, axis=0, keepdims=True)
        r0 = pl.multiple_of(h * P_TOPK, P_TOPK)
        e_ref[half, pl.ds(r0, P_TOPK), :] = eidx
        g_ref[half, pl.ds(r0, P_TOPK), :] = gate
        return carry

    lax.fori_loop(0, P_HEADS * n_half, unit, 0)


def _route(h1, wq_t, sk, tm):
    m, d = h1.shape
    n_half = tm // LANES
    nk = P_HEADS * P_TOPK
    kern = functools.partial(_route_kernel, n_half=n_half)
    return pl.pallas_call(
        kern,
        out_shape=(jax.ShapeDtypeStruct((m // LANES, nk, LANES), I32),
                   jax.ShapeDtypeStruct((m // LANES, nk, LANES), F32)),
        grid=(m // tm,),
        in_specs=[pl.BlockSpec((tm, d), lambda i: (i, 0)),
                  pl.BlockSpec(wq_t.shape, lambda i: (0, 0)),
                  pl.BlockSpec(sk.shape, lambda i: (0, 0, 0, 0))],
        out_specs=(pl.BlockSpec((n_half, nk, LANES), lambda i: (i, 0, 0)),
                   pl.BlockSpec((n_half, nk, LANES), lambda i: (i, 0, 0))),
        scratch_shapes=[pltpu.VMEM((n_half, wq_t.shape[0], LANES), BF16)],
        compiler_params=_cparams("parallel"),
        name="route",
    )(h1, wq_t, sk)


def _gelu_tanh(x):
    return 0.5 * x * (1.0 + jnp.tanh(math.sqrt(2.0 / math.pi) * (x + 0.044715 * (x * x * x))))


def _sublane_sums(rs):
    sub = lax.broadcasted_iota(I32, (SUBLANES, LANES), 0)
    lo4 = sub < 4
    lo2 = (sub % 4) < 2
    even = (sub % 2) == 0

    def lvl1(a, b):
        return jnp.where(lo4, a, b) + pltpu.roll(jnp.where(lo4, b, a), 4, 0)

    def lvl(a, b, mask, d):
        return jnp.where(mask, a + pltpu.roll(a, SUBLANES - d, 0), b + pltpu.roll(b, d, 0))

    c04, c26, c15, c37 = lvl1(rs[0], rs[4]), lvl1(rs[2], rs[6]), lvl1(rs[1], rs[5]), lvl1(rs[3], rs[7])
    return lvl(lvl(c04, c26, lo2, 2), lvl(c15, c37, lo2, 2), even, 1)


def _apply_kernel(idx_ref, idxn_ref, x_ref, gate_ref, g_ref, b_ref, uv_hbm, o_ref, *scratch,
                  tt, nk, dsub):
    buf = scratch[:tt]
    cb, sem = scratch[tt:tt + 2], scratch[tt + 2]
    i = pl.program_id(0)
    n = pl.num_programs(0)
    lane = lax.broadcasted_iota(I32, (nk, LANES), 1)
    lane0 = (i * tt) % LANES
    cnt = float(dsub * LANES)

    def issue(src_idx, t, slot):
        for k in range(nk):
            pltpu.make_async_copy(uv_hbm.at[src_idx[t, k]], buf[slot].at[k],
                                  sem.at[slot]).start(priority=k % 2)

    def wait(slot):
        pltpu.make_async_copy(uv_hbm.at[pl.ds(0, nk)], buf[slot], sem.at[slot]).wait()

    def scores(j):
        x = x_ref[j]
        gcol = jnp.sum(jnp.where(lane == lane0 + j, gate_ref[0], 0.0), axis=1, keepdims=True)
        zs = []
        for g in range(nk // SUBLANES):
            rs = []
            for k in range(SUBLANES):
                p = buf[j][g * SUBLANES + k, 0:dsub, :].astype(F32) * x
                r = p[0:SUBLANES]
                for c in range(1, dsub // SUBLANES):
                    r = r + p[c * SUBLANES:(c + 1) * SUBLANES]
                rs.append(r)
            zs.append(_sublane_sums(rs))
        act = jnp.sum(jnp.concatenate(zs, axis=0), axis=1, keepdims=True)
        cb[j % 2][...] = jnp.broadcast_to(_gelu_tanh(act) * gcol, (nk, LANES))

    def mix(j):
        x = x_ref[j]
        accs = [None] * 4
        for k in range(nk):
            term = buf[j][k, dsub:2 * dsub, :].astype(F32) * cb[j % 2][k:k + 1, :]
            accs[k % 4] = term if accs[k % 4] is None else accs[k % 4] + term
        r = ALPHA * x + ((accs[0] + accs[1]) + (accs[2] + accs[3]))
        mu = jnp.sum(jnp.sum(r, axis=1, keepdims=True), axis=0, keepdims=True) / cnt
        rc = r - mu
        var = jnp.sum(jnp.sum(rc * rc, axis=1, keepdims=True), axis=0, keepdims=True) / cnt
        o_ref[j] = rc * lax.rsqrt(var + EPS) * g_ref[...] + b_ref[...]

    @pl.when(i == 0)
    def _prologue():
        for j in range(tt - 1):
            issue(idx_ref, j, j)

    def body(last):
        wait(0)
        issue(idx_ref, tt - 1, tt - 1)
        scores(0)
        for j in range(tt - 1):
            wait(j + 1)
            if j >= 1 and not last:
                issue(idxn_ref, j - 1, j - 1)
            mix(j)
            scores(j + 1)
        mix(tt - 1)
        if not last:
            issue(idxn_ref, tt - 2, tt - 2)

    @pl.when(i + 1 < n)
    def _steady():
        body(False)

    @pl.when(i + 1 == n)
    def _final():
        body(True)


def _apply(idx, h1_3, gates, g3, b3, uv3, tt):
    m, nk = idx.shape
    dsub = h1_3.shape[1]
    n_steps = m // tt
    kern = functools.partial(_apply_kernel, tt=tt, nk=nk, dsub=dsub)
    return pl.pallas_call(
        kern,
        out_shape=jax.ShapeDtypeStruct(h1_3.shape, F32),
        grid=(n_steps,),
        in_specs=[pl.BlockSpec((tt, nk), lambda i: (i, 0), memory_space=pltpu.SMEM),
                  pl.BlockSpec((tt, nk), lambda i: (jnp.minimum(i + 1, n_steps - 1), 0),
                               memory_space=pltpu.SMEM),
                  pl.BlockSpec((tt, dsub, LANES), lambda i: (i, 0, 0)),
                  pl.BlockSpec((1, nk, LANES), lambda i: (i * tt // LANES, 0, 0)),
                  pl.BlockSpec((dsub, LANES), lambda i: (0, 0)),
                  pl.BlockSpec((dsub, LANES), lambda i: (0, 0)),
                  pl.BlockSpec(memory_space=pl.ANY)],
        out_specs=pl.BlockSpec((tt, dsub, LANES), lambda i: (i, 0, 0)),
        scratch_shapes=[pltpu.VMEM((nk, 2 * dsub, LANES), BF16) for _ in range(tt)]
        + [pltpu.VMEM((nk, LANES), F32), pltpu.VMEM((nk, LANES), F32), pltpu.SemaphoreType.DMA((tt,))],
        compiler_params=_cparams("arbitrary"),
        name="peer_apply",
    )(idx, idx, h1_3, gates, g3, b3, uv3)


def _regroup_w_in(w):
    d = w.shape[0]
    widths = (A_HEADS * A_HEAD_DIM, A_KV_RANK, IDX_HEADS * IDX_DIM, IDX_DIM, IDX_HEADS,
              G_HEADS * G_KEY_DIM, G_HEADS * G_KEY_DIM, G_HEADS * G_VAL_DIM, G_GATE_RANK,
              G_HEADS * G_VAL_DIM)
    offs = np.concatenate([[0], np.cumsum(widths)])
    a_q, a_ckv, i_q, i_k, i_w, g_q, g_k, g_v, g_r, g_o = [w[:, offs[n]:offs[n + 1]] for n in range(10)]
    z = lambda n: jnp.zeros((d, n), w.dtype)
    cat = jnp.concatenate([a_q, i_q, g_v, g_o, g_q, g_k, a_ckv,
                           i_k, i_w, z(LANES - IDX_DIM - IDX_HEADS),
                           g_r, z(LANES - G_GATE_RANK)], axis=1)
    assert cat.shape[1] == N_COLS
    return cat.astype(BF16)


def _dsa_groups(n_qblocks, n_groups):
    per = -(-n_qblocks // n_groups)
    return [(lo, min(per, n_qblocks - lo)) for lo in range(0, n_qblocks, per)]


def kernel(x, meta_tokens, ln0_g, ln0_b, rel_bias, w_in, w_uk, w_uv, w_gk2, b_gk, gla_norm_g, w_out,
           ln1_g, ln1_b, w_pq, sub_keys, u_tab, v_tab, ln2_g, ln2_b):
    bsz, s, d = x.shape
    t = bsz * s
    topk = min(TOPK_MAX, s // 4)
    assert s % Q_BLOCK == 0 and d % (SUBLANES * LANES) == 0
    _check_bucket_saturation(s + N_META)
    row2 = lambda v: v.reshape(1, -1)

    wcat = _regroup_w_in(w_in[0])
    h, proj = _ln_proj(x.reshape(t, d), row2(ln0_g), row2(ln0_b), wcat, tm=512, tn=512)
    _, proj_meta = _ln_proj(meta_tokens, row2(ln0_g), row2(ln0_b), wcat, tm=N_META, tn=512)
    proj3 = proj.reshape(bsz, s, N_COLS)

    pad_rows = lambda a: jnp.pad(a, ((0, LANES - N_META), (0, 0)))
    ckv_meta = pad_rows(proj_meta[:, C_CKV:C_CKV + A_KV_RANK])
    idx_meta = pad_rows(proj_meta[:, C_IDX:C_IDX + LANES])
    btab = _bias_tab(rel_bias)
    wuk = w_uk[0].astype(BF16)
    wuv = w_uv[0].astype(BF16)
    ya_parts = []
    for q_lo, n_q in _dsa_groups(s // Q_BLOCK, 8):
        width = (q_lo + n_q) * Q_BLOCK
        ya_parts.append(_dsa(proj3, ckv_meta, idx_meta, wuk, wuv, btab, q_lo, n_q, width, topk))
    y_a = jnp.concatenate(ya_parts, axis=1).reshape(t, A_HEADS * A_HEAD_DIM)

    y_b = _gla(proj3, proj_meta, w_gk2[0], row2(b_gk[0]), row2(gla_norm_g[0])).reshape(t, -1)

    wo = w_out[0].astype(BF16)
    ka = A_HEADS * A_HEAD_DIM
    h1 = _out_ln(y_a, y_b, h, wo[:ka], wo[ka:], row2(ln1_g[0]), row2(ln1_b[0]), tm=256)

    wq_t = w_pq[0].T.astype(BF16)
    eidx, gates = _route(h1, wq_t, sub_keys[0].astype(BF16), tm=256)
    nk = P_HEADS * P_TOPK
    idx = eidx.transpose(0, 2, 1).reshape(t, nk)
    dsub = d // LANES
    uv3 = jnp.concatenate([u_tab[0].reshape(-1, dsub, LANES), v_tab[0].reshape(-1, dsub, LANES)],
                          axis=1).astype(BF16)
    out3 = _apply(idx, h1.reshape(t, dsub, LANES), gates, ln2_g[0].reshape(dsub, LANES),
                  ln2_b[0].reshape(dsub, LANES), uv3, tt=16)
    return out3.reshape(bsz, s, d)
```

```python
import functools
import math

import numpy as np
import jax
import jax.numpy as jnp
from jax import lax
from jax.experimental import pallas as pl
from jax.experimental.pallas import tpu as pltpu

F32 = jnp.float32
BF16 = jnp.bfloat16
I32 = jnp.int32

N_META = 16
CHUNK = 64
SUB_BLOCK = 16
EPS = 1e-5
DEPTH = 1
ALPHA = (2 * DEPTH) ** 0.25
A_HEADS = 8
A_HEAD_DIM = 128
A_KV_RANK = 256
IDX_HEADS = 16
IDX_DIM = 64
TOPK_MAX = 256
Q_BLOCK = 128
REL_BUCKETS = 32
REL_MAX_DIST = 128
G_HEADS = 4
G_VAL_DIM = 256
G_KEY_DIM = 128
G_GATE_RANK = 16
G_GATE_NORM = 16.0
P_HEADS = 8
P_NKEYS = 128
P_QDIM = 256
P_TOPK = 16

LANES = 128
SUBLANES = 8
VMEM_LIMIT = 56 * 1024 * 1024

C_AQ, C_IQ, C_GV, C_GO = 0, 1024, 2048, 3072
C_GQ, C_GK = 4096, 4608
C_CKV = 5120
C_IDX = 5376
C_GR = 5504
N_COLS = 5632

INT_MIN = -2 ** 31
NT_DIMS = (((1,), (1,)), ((), ()))
TN_DIMS = (((0,), (0,)), ((), ()))


def _cparams(*sem):
    return pltpu.CompilerParams(dimension_semantics=sem, vmem_limit_bytes=VMEM_LIMIT)


def _layer_norm(x, g, b):
    mu = jnp.mean(x, axis=-1, keepdims=True)
    xc = x - mu
    var = jnp.mean(xc * xc, axis=-1, keepdims=True)
    return xc * lax.rsqrt(var + EPS) * g + b


def _ln_proj_kernel(x_ref, g_ref, b_ref, w_ref, h_ref, p_ref, hb_ref):
    @pl.when(pl.program_id(1) == 0)
    def _():
        y = _layer_norm(x_ref[...], g_ref[...], b_ref[...])
        h_ref[...] = y
        hb_ref[...] = y.astype(BF16)

    p_ref[...] = jnp.dot(hb_ref[...], w_ref[...], preferred_element_type=F32)


def _ln_proj(x2, g, b, wcat, tm, tn):
    m, d = x2.shape
    n = wcat.shape[1]
    return pl.pallas_call(
        _ln_proj_kernel,
        out_shape=(jax.ShapeDtypeStruct((m, d), F32), jax.ShapeDtypeStruct((m, n), F32)),
        grid=(m // tm, n // tn),
        in_specs=[pl.BlockSpec((tm, d), lambda i, j: (i, 0)),
                  pl.BlockSpec((1, d), lambda i, j: (0, 0)),
                  pl.BlockSpec((1, d), lambda i, j: (0, 0)),
                  pl.BlockSpec((d, tn), lambda i, j: (0, j))],
        out_specs=(pl.BlockSpec((tm, d), lambda i, j: (i, 0)),
                   pl.BlockSpec((tm, tn), lambda i, j: (i, j))),
        scratch_shapes=[pltpu.VMEM((tm, d), BF16)],
        compiler_params=_cparams("parallel", "arbitrary"),
        name="ln_proj",
    )(x2, g, b, wcat)


N_PAT = 5


def _t5_bucket(rel):
    nb = REL_BUCKETS // 2
    max_exact = nb // 2
    base = jnp.where(rel > 0, nb, 0)
    n = jnp.abs(rel)
    nf = jnp.maximum(n, 1).astype(jnp.float32)
    large = max_exact + (jnp.log(nf / max_exact) / math.log(REL_MAX_DIST / max_exact)
                         * (nb - max_exact)).astype(jnp.int32)
    large = jnp.minimum(large, nb - 1)
    return base + jnp.where(n < max_exact, n, large)


def _bias_patterns():
    i = jnp.arange(Q_BLOCK, dtype=I32)[:, None]
    j = jnp.arange(LANES, dtype=I32)[None, :]
    rels = [j - i, j - i - LANES, j - i - 2 * LANES, j - N_META - i, j - N_META - i - LANES]
    return jnp.stack([_t5_bucket(r) for r in rels])


def _check_bucket_saturation(max_dist):
    nb = REL_BUCKETS // 2
    max_exact = nb // 2
    n = np.arange(LANES + 1, max_dist + 1, dtype=np.float64)
    large = max_exact + np.floor(np.log(n / max_exact) / math.log(REL_MAX_DIST / max_exact) * (nb - max_exact))
    assert np.all(large >= nb - 1 + 0.5), "relative-position buckets do not saturate within one key tile"


def _bias_tab_kernel(bk_ref, rb_ref, o_ref):
    h = pl.program_id(0)
    for p in range(N_PAT):
        bk = bk_ref[p]
        acc = jnp.zeros((Q_BLOCK, LANES), F32)
        for b in range(REL_BUCKETS):
            acc = jnp.where(bk == b, rb_ref[b, h], acc)
        o_ref[0, p] = acc


def _bias_tab(rel_bias):
    return pl.pallas_call(
        _bias_tab_kernel,
        out_shape=jax.ShapeDtypeStruct((A_HEADS, N_PAT, Q_BLOCK, LANES), F32),
        grid=(A_HEADS,),
        in_specs=[pl.BlockSpec((N_PAT, Q_BLOCK, LANES), lambda h: (0, 0, 0)),
                  pl.BlockSpec(memory_space=pltpu.SMEM)],
        out_specs=pl.BlockSpec((1, N_PAT, Q_BLOCK, LANES), lambda h: (h, 0, 0, 0)),
        compiler_params=_cparams("arbitrary"),
        name="bias_tab",
    )(_bias_patterns(), rel_bias)


def _dsa_kernel(aq_ref, iq_ref, idxq_ref, ckv_ref, idxk_ref, ckvm_ref, idxm_ref, wuk_ref, wuv_ref,
                bt_ref, o_ref,
                ik_scr, ckvb_scr, iqs_scr, l_scr, key_scr, el_scr, madd_scr, ql_scr, lg_scr, p_scr,
                den_scr, *, q_lo, width, topk):
    nt = 1 + width // LANES
    wt = nt * LANES
    qi = pl.program_id(1) + q_lo

    @pl.when(pl.program_id(1) == 0)
    def _stage_keys():
        ik_scr[0:LANES, :] = idxm_ref[:, 0:IDX_DIM].astype(BF16)
        ik_scr[LANES:wt, :] = idxk_ref[0, :, 0:IDX_DIM].astype(BF16)
        ckvb_scr[0:LANES, :] = ckvm_ref[...].astype(BF16)
        ckvb_scr[LANES:wt, :] = ckv_ref[0].astype(BF16)

    row = lax.broadcasted_iota(I32, (Q_BLOCK, LANES), 0)
    lane = lax.broadcasted_iota(I32, (Q_BLOCK, LANES), 1)
    kb = ((qi * Q_BLOCK + row) // CHUNK + 1) * CHUNK

    iq = iq_ref[0].astype(BF16)
    for h in range(IDX_HEADS):
        iqs_scr[h * Q_BLOCK:(h + 1) * Q_BLOCK, :] = iq[:, h * IDX_DIM:(h + 1) * IDX_DIM]
    iw = idxq_ref[0][:, IDX_DIM:IDX_DIM + IDX_HEADS] * (IDX_HEADS ** -0.5 * IDX_DIM ** -0.5)
    iw_b = [jnp.broadcast_to(iw[:, h:h + 1], (Q_BLOCK, LANES)) for h in range(IDX_HEADS)]
    for kt in range(nt):
        l_scr[...] = lax.dot_general(iqs_scr[...], ik_scr[kt * LANES:(kt + 1) * LANES, :], NT_DIMS,
                                     preferred_element_type=F32)
        acc = jnp.zeros((Q_BLOCK, LANES), F32)
        for h in range(IDX_HEADS):
            acc = acc + jnp.maximum(l_scr[h * Q_BLOCK:(h + 1) * Q_BLOCK, :], 0.0) * iw_b[h]
        adm = (lane < N_META) if kt == 0 else (lane + (kt - 1) * LANES < kb)
        bits = pltpu.bitcast(acc + 0.0, I32)
        key = jnp.where(bits < 0, bits ^ 0x7FFFFFFF, bits)
        key_scr[:, kt * LANES:(kt + 1) * LANES] = jnp.where(adm, key, INT_MIN)

    def count(pred):
        acc = jnp.zeros((Q_BLOCK, LANES), F32)
        for kt in range(nt):
            acc = acc + jnp.where(pred(kt), 1.0, 0.0)
        return jnp.sum(acc, axis=1, keepdims=True)

    def key_tile(kt):
        return key_scr[:, kt * LANES:(kt + 1) * LANES]

    def val_step(it, thr):
        cand = thr + jnp.left_shift(jnp.int32(1), 31 - it)
        cnt = count(lambda kt: key_tile(kt) >= cand)
        return jnp.where(cnt >= topk, cand, thr)

    thr = lax.fori_loop(0, 32, val_step, jnp.full((Q_BLOCK, 1), INT_MIN, I32))
    n_ge = count(lambda kt: key_tile(kt) >= thr)
    thr_adm = jnp.maximum(thr, INT_MIN + 1)
    for kt in range(nt):
        madd_scr[:, kt * LANES:(kt + 1) * LANES] = jnp.where(key_tile(kt) >= thr_adm, 0.0, -jnp.inf)
    tied = jnp.where((n_ge > topk) & (thr > INT_MIN), 1.0, 0.0)

    @pl.when(jnp.max(tied) > 0.0)
    def _break_ties():
        need = topk - count(lambda kt: key_tile(kt) > thr)
        big = jnp.int32(2 ** 30)
        for kt in range(nt):
            el_scr[:, kt * LANES:(kt + 1) * LANES] = jnp.where(key_tile(kt) == thr, lane + kt * LANES, big)

        def el_tile(kt):
            return el_scr[:, kt * LANES:(kt + 1) * LANES]

        nbits = max(1, int(wt - 1).bit_length())

        def pos_step(it, x):
            cand = x + jnp.left_shift(jnp.int32(1), nbits - 1 - it)
            cnt = count(lambda kt: el_tile(kt) < cand)
            return jnp.where(cnt < need, cand, x)

        xpos = lax.fori_loop(0, nbits, pos_step, jnp.zeros((Q_BLOCK, 1), I32))
        for kt in range(nt):
            sel = ((key_tile(kt) > thr) | (el_tile(kt) <= xpos)) & (key_tile(kt) > INT_MIN)
            madd_scr[:, kt * LANES:(kt + 1) * LANES] = jnp.where(sel, 0.0, -jnp.inf)

    aq = aq_ref[0].astype(BF16)
    for h in range(A_HEADS):
        qh = lax.dot_general(aq[:, h * A_HEAD_DIM:(h + 1) * A_HEAD_DIM], wuk_ref[h], NT_DIMS,
                             preferred_element_type=F32) * (A_HEAD_DIM ** -0.5)
        ql_scr[h * Q_BLOCK:(h + 1) * Q_BLOCK, :] = qh.astype(BF16)
    for kt in range(nt):
        lg_scr[:, kt * LANES:(kt + 1) * LANES] = lax.dot_general(
            ql_scr[...], ckvb_scr[kt * LANES:(kt + 1) * LANES, :], NT_DIMS, preferred_element_type=F32)

    def head_body(h, carry):
        r0 = pl.multiple_of(h * Q_BLOCK, Q_BLOCK)
        rows = pl.ds(r0, Q_BLOCK)
        m = jnp.full((Q_BLOCK, LANES), -jnp.inf, F32)
        for kt in range(nt):
            cols = slice(kt * LANES, (kt + 1) * LANES)
            pat = (3 + jnp.minimum(qi, 1)) if kt == 0 else jnp.clip(qi - (kt - 1), 0, 2)
            l = lg_scr[rows, cols] + bt_ref[h, pat] + madd_scr[:, cols]
            lg_scr[rows, cols] = l
            m = jnp.maximum(m, l)
        mrow = jnp.max(m, axis=1, keepdims=True)
        ssum = jnp.zeros((Q_BLOCK, LANES), F32)
        for kt in range(nt):
            cols = slice(kt * LANES, (kt + 1) * LANES)
            e = jnp.exp(lg_scr[rows, cols] - mrow)
            ssum = ssum + e
            p_scr[rows, cols] = e.astype(BF16)
        den_scr[rows, :] = jnp.broadcast_to(jnp.sum(ssum, axis=1, keepdims=True), (Q_BLOCK, LANES))
        return carry

    lax.fori_loop(0, A_HEADS, head_body, 0)
    o = jnp.dot(p_scr[...], ckvb_scr[...], preferred_element_type=F32)
    for h in range(A_HEADS):
        rs = slice(h * Q_BLOCK, (h + 1) * Q_BLOCK)
        oh = o[rs] / den_scr[rs, 0:1]
        o_ref[0, :, h * A_HEAD_DIM:(h + 1) * A_HEAD_DIM] = jnp.dot(
            oh.astype(BF16), wuv_ref[h], preferred_element_type=F32)


def _dsa(proj3, ckv_meta, idx_meta, wuk, wuv, btab, q_lo, n_q, width, topk):
    bsz, s, _ = proj3.shape
    nt = 1 + width // LANES
    wt = nt * LANES
    hq = A_HEADS * Q_BLOCK
    kern = functools.partial(_dsa_kernel, q_lo=q_lo, width=width, topk=topk)
    return pl.pallas_call(
        kern,
        out_shape=jax.ShapeDtypeStruct((bsz, n_q * Q_BLOCK, A_HEADS * A_HEAD_DIM), F32),
        grid=(bsz, n_q),
        in_specs=[
            pl.BlockSpec((1, Q_BLOCK, 1024), lambda b, q: (b, q + q_lo, C_AQ // 1024)),
            pl.BlockSpec((1, Q_BLOCK, 1024), lambda b, q: (b, q + q_lo, C_IQ // 1024)),
            pl.BlockSpec((1, Q_BLOCK, LANES), lambda b, q: (b, q + q_lo, C_IDX // LANES)),
            pl.BlockSpec((1, width, A_KV_RANK), lambda b, q: (b, 0, C_CKV // A_KV_RANK)),
            pl.BlockSpec((1, width, LANES), lambda b, q: (b, 0, C_IDX // LANES)),
            pl.BlockSpec((LANES, A_KV_RANK), lambda b, q: (0, 0)),
            pl.BlockSpec((LANES, LANES), lambda b, q: (0, 0)),
            pl.BlockSpec((A_HEADS, A_KV_RANK, A_HEAD_DIM), lambda b, q: (0, 0, 0)),
            pl.BlockSpec((A_HEADS, A_KV_RANK, A_HEAD_DIM), lambda b, q: (0, 0, 0)),
            pl.BlockSpec((A_HEADS, N_PAT, Q_BLOCK, LANES), lambda b, q: (0, 0, 0, 0)),
        ],
        out_specs=pl.BlockSpec((1, Q_BLOCK, A_HEADS * A_HEAD_DIM), lambda b, q: (b, q, 0)),
        scratch_shapes=[
            pltpu.VMEM((wt, IDX_DIM), BF16),
            pltpu.VMEM((wt, A_KV_RANK), BF16),
            pltpu.VMEM((IDX_HEADS * Q_BLOCK, IDX_DIM), BF16),
            pltpu.VMEM((IDX_HEADS * Q_BLOCK, LANES), F32),
            pltpu.VMEM((Q_BLOCK, wt), I32),
            pltpu.VMEM((Q_BLOCK, wt), I32),
            pltpu.VMEM((Q_BLOCK, wt), F32),
            pltpu.VMEM((hq, A_KV_RANK), BF16),
            pltpu.VMEM((hq, wt), F32),
            pltpu.VMEM((hq, wt), BF16),
            pltpu.VMEM((hq, LANES), F32),
        ],
        compiler_params=_cparams("parallel", "arbitrary"),
        name=f"dsa_w{width}",
    )(proj3, proj3, proj3, proj3, proj3, ckv_meta, idx_meta, wuk, wuv, btab)


def _gla_kernel(gq_ref, gk_ref, gv_ref, go_ref, gr_ref, gkm_ref, gvm_ref, grm_ref, w2_ref, bg_ref,
                ng_ref, o_ref, st_scr):
    w2 = w2_ref[...].astype(BF16)

    def log_decay(gr):
        z = jnp.dot(gr.astype(BF16), w2, preferred_element_type=F32) + bg_ref[...]
        return (jnp.minimum(z, 0.0) - jnp.log(1.0 + jnp.exp(-jnp.abs(z)))) * (1.0 / G_GATE_NORM)

    def cumsum_rows(g):
        n = g.shape[0]
        tri = (lax.broadcasted_iota(I32, (n, n), 0) >= lax.broadcasted_iota(I32, (n, n), 1)).astype(F32)
        return jnp.dot(tri, g, precision=lax.Precision.HIGHEST, preferred_element_type=F32)

    @pl.when(pl.program_id(1) == 0)
    def _init_state():
        bm = cumsum_rows(log_decay(grm_ref[:, 0:G_GATE_RANK]))
        kd = gkm_ref[...] * jnp.exp(bm[N_META - 1:N_META, :] - bm)
        for h in range(G_HEADS):
            st_scr[h] = lax.dot_general(
                gvm_ref[:, h * G_VAL_DIM:(h + 1) * G_VAL_DIM].astype(BF16),
                kd[:, h * G_KEY_DIM:(h + 1) * G_KEY_DIM].astype(BF16), TN_DIMS,
                preferred_element_type=F32)

    b = cumsum_rows(log_decay(gr_ref[0][:, 0:G_GATE_RANK]))
    row = lax.broadcasted_iota(I32, (CHUNK, CHUNK), 0)
    col = lax.broadcasted_iota(I32, (CHUNK, CHUNK), 1)
    col_sb = lax.broadcasted_iota(I32, (SUB_BLOCK, CHUNK), 1)
    for h in range(G_HEADS):
        ks = slice(h * G_KEY_DIM, (h + 1) * G_KEY_DIM)
        vs = slice(h * G_VAL_DIM, (h + 1) * G_VAL_DIM)
        q = gq_ref[0][:, ks] * (G_KEY_DIM ** -0.5)
        k = gk_ref[0][:, ks]
        v = gv_ref[0][:, vs].astype(BF16)
        bh = b[:, ks]
        st = st_scr[h]
        o = lax.dot_general((q * jnp.exp(bh)).astype(BF16), st.astype(BF16), NT_DIMS,
                            preferred_element_type=F32)
        blocks = []
        for ib in range(CHUNK // SUB_BLOCK):
            rs = slice(ib * SUB_BLOCK, (ib + 1) * SUB_BLOCK)
            qi, bi = q[rs], bh[rs]
            ai = jnp.zeros((SUB_BLOCK, CHUNK), F32)
            if ib > 0:
                br = bi[0:1, :]
                kt = k * jnp.exp(jnp.minimum(br - bh, 0.0))
                ai = lax.dot_general(qi * jnp.exp(bi - br), kt, NT_DIMS, precision=lax.Precision.HIGHEST,
                                     preferred_element_type=F32)
                ai = jnp.where(col_sb < ib * SUB_BLOCK, ai, 0.0)
            for j in range(ib * SUB_BLOCK, (ib + 1) * SUB_BLOCK):
                w = qi * k[j:j + 1, :] * jnp.exp(jnp.minimum(bi - bh[j:j + 1, :], 0.0))
                ai = jnp.where(col_sb == j, jnp.sum(w, axis=1, keepdims=True), ai)
            blocks.append(ai)
        a = jnp.concatenate(blocks, axis=0)
        a = jnp.where(row >= col, a, 0.0)
        o = o + jnp.dot(a.astype(BF16), v, preferred_element_type=F32)
        bl = bh[CHUNK - 1:CHUNK, :]
        kd = k * jnp.exp(bl - bh)
        st_scr[h] = st * jnp.exp(bl) + lax.dot_general(v, kd.astype(BF16), TN_DIMS,
                                                       preferred_element_type=F32)
        on = o * lax.rsqrt(jnp.mean(o * o, axis=1, keepdims=True) + EPS) * ng_ref[...]
        og = go_ref[0][:, vs]
        o_ref[0, :, vs] = on * (og * (1.0 / (1.0 + jnp.exp(-og))))


def _gla(proj3, proj_meta, w_gk2, b_gk, norm_g):
    bsz, s, _ = proj3.shape
    hk = G_HEADS * G_KEY_DIM
    hv = G_HEADS * G_VAL_DIM
    gkm = proj_meta[:, C_GK:C_GK + hk]
    gvm = proj_meta[:, C_GV:C_GV + hv]
    grm = proj_meta[:, C_GR:C_GR + LANES]
    full = lambda shp: pl.BlockSpec(shp, lambda b, c: (0,) * len(shp))
    return pl.pallas_call(
        _gla_kernel,
        out_shape=jax.ShapeDtypeStruct((bsz, s, hv), F32),
        grid=(bsz, s // CHUNK),
        in_specs=[
            pl.BlockSpec((1, CHUNK, hk), lambda b, c: (b, c, C_GQ // hk)),
            pl.BlockSpec((1, CHUNK, hk), lambda b, c: (b, c, C_GK // hk)),
            pl.BlockSpec((1, CHUNK, hv), lambda b, c: (b, c, C_GV // hv)),
            pl.BlockSpec((1, CHUNK, hv), lambda b, c: (b, c, C_GO // hv)),
            pl.BlockSpec((1, CHUNK, LANES), lambda b, c: (b, c, C_GR // LANES)),
            full((N_META, hk)), full((N_META, hv)), full((N_META, LANES)),
            full((G_GATE_RANK, hk)), full((1, hk)), full((1, G_VAL_DIM)),
        ],
        out_specs=pl.BlockSpec((1, CHUNK, hv), lambda b, c: (b, c, 0)),
        scratch_shapes=[pltpu.VMEM((G_HEADS, G_VAL_DIM, G_KEY_DIM), F32)],
        compiler_params=_cparams("parallel", "arbitrary"),
        name="gla",
    )(proj3, proj3, proj3, proj3, proj3, gkm, gvm, grm, w_gk2, b_gk, norm_g)


def _out_ln_kernel(ya_ref, yb_ref, h_ref, wa_ref, wb_ref, g_ref, b_ref, o_ref):
    mix = (jnp.dot(ya_ref[...].astype(BF16), wa_ref[...], preferred_element_type=F32)
           + jnp.dot(yb_ref[...].astype(BF16), wb_ref[...], preferred_element_type=F32))
    o_ref[...] = _layer_norm(ALPHA * h_ref[...] + mix, g_ref[...], b_ref[...])


def _out_ln(ya, yb, h, wa, wb, g, b, tm):
    m, d = h.shape
    ka, kb = ya.shape[1], yb.shape[1]
    return pl.pallas_call(
        _out_ln_kernel,
        out_shape=jax.ShapeDtypeStruct((m, d), F32),
        grid=(m // tm,),
        in_specs=[pl.BlockSpec((tm, ka), lambda i: (i, 0)),
                  pl.BlockSpec((tm, kb), lambda i: (i, 0)),
                  pl.BlockSpec((tm, d), lambda i: (i, 0)),
                  pl.BlockSpec((ka, d), lambda i: (0, 0)),
                  pl.BlockSpec((kb, d), lambda i: (0, 0)),
                  pl.BlockSpec((1, d), lambda i: (0, 0)),
                  pl.BlockSpec((1, d), lambda i: (0, 0))],
        out_specs=pl.BlockSpec((tm, d), lambda i: (i, 0)),
        compiler_params=_cparams("parallel"),
        name="out_ln",
    )(ya, yb, h, wa, wb, g, b)


RANK_NONE = 2 ** 20


def _top_rows(s, rank, k):
    slot = lax.broadcasted_iota(I32, (k, LANES), 0)
    vals = jnp.zeros((k, LANES), F32)
    idxs = jnp.zeros((k, LANES), I32)
    for r in range(k):
        m = jnp.max(s, axis=0, keepdims=True)
        im = jnp.min(jnp.where(s == m, rank, RANK_NONE), axis=0, keepdims=True)
        vals = jnp.where(slot == r, m, vals)
        idxs = jnp.where(slot == r, im, idxs)
        s = jnp.where(rank == im, -jnp.inf, s)
    return vals, idxs


def _product_candidates(s1, i1, s2, i2):
    sub = lax.broadcasted_iota(I32, (SUBLANES, LANES), 0)
    lo = sub < 4
    s2a, s2b = s2[0:SUBLANES], s2[SUBLANES:2 * SUBLANES]
    i2a, i2b = i2[0:SUBLANES], i2[SUBLANES:2 * SUBLANES]
    s2d = jnp.where(lo, s2a, pltpu.roll(s2a, 4, 0))
    i2d = jnp.where(lo, i2a, pltpu.roll(i2a, 4, 0))
    row = lambda v, a: jnp.broadcast_to(v[a:a + 1, :], (SUBLANES, LANES))
    pair = lambda v, a: jnp.where(lo, row(v, a), row(v, a + 1))
    e1 = i1 * P_NKEYS
    groups = [
        (row(s1, 0) + s2a, row(e1, 0) + i2a, sub, None),
        (row(s1, 0) + s2b, row(e1, 0) + i2b, sub + SUBLANES, None),
        (row(s1, 1) + s2a, row(e1, 1) + i2a, sub + P_TOPK, None),
        (row(s1, 2) + s2a, row(e1, 2) + i2a, sub + 2 * P_TOPK, sub < 5),
        (row(s1, 3) + s2a, row(e1, 3) + i2a, sub + 3 * P_TOPK, sub < 4),
        (pair(s1, 4) + s2d, pair(e1, 4) + i2d, jnp.where(lo, sub + 4 * P_TOPK, sub - 4 + 5 * P_TOPK),
         (sub < 3) | ((sub >= 4) & (sub < 6))),
        (pair(s1, 6) + s2d, pair(e1, 6) + i2d, jnp.where(lo, sub + 6 * P_TOPK, sub - 4 + 7 * P_TOPK),
         (sub < 2) | ((sub >= 4) & (sub < 6))),
        (s1[SUBLANES:] + row(s2, 0), e1[SUBLANES:] + row(i2, 0), (sub + SUBLANES) * P_TOPK, None),
    ]
    vals, eids, ranks = [], [], []
    for v, e, rk, ok in groups:
        vals.append(v if ok is None else jnp.where(ok, v, -jnp.inf))
        ranks.append(rk if ok is None else jnp.where(ok, rk, RANK_NONE))
        eids.append(e)
    cat = lambda xs: jnp.concatenate(xs, axis=0)
    return cat(vals), cat(ranks), cat(eids)


def _route_kernel(h_ref, wq_ref, sk_ref, e_ref, g_ref, qt_scr, *, n_half):
    qt = lax.dot_general(wq_ref[...], h_ref[...].astype(BF16), NT_DIMS, preferred_element_type=F32)
    for half in range(n_half):
        qt_scr[half] = qt[:, half * LANES:(half + 1) * LANES].astype(BF16)
    key_id = lax.broadcasted_iota(I32, (P_NKEYS, LANES), 0)
    slot = lax.broadcasted_iota(I32, (P_TOPK, LANES), 0)
    half_dim = P_QDIM // 2

    def one_head(h, carry):
        for half in range(n_half):
            unit(h, half)
        return carry

    def unit(h, half):
        tops = []
        for c in range(2):
            d0 = pl.multiple_of((h * 2 + c) * half_dim, half_dim)
            s = jnp.dot(sk_ref[h, c], qt_scr[half, pl.ds(d0, half_dim), :], preferred_element_type=F32)
            tops.append(_top_rows(s, key_id, P_TOPK))
        (s1, i1), (s2, i2) = tops
        cand, crank, cidx = _product_candidates(s1, i1, s2, i2)
        top, pos = _top_rows(cand, crank, P_TOPK)
        eidx = jnp.zeros((P_TOPK, LANES), I32)
        for r in range(P_TOPK):
            er = jnp.max(jnp.where(crank == pos[r:r + 1, :], cidx, -1), axis=0, keepdims=True)
            eidx = jnp.where(slot == r, er, eidx)
        ex = jnp.exp(top - top[0:1, :])
        gate = ex / jnp.sum(ex, axis=0, keepdims=True)
        r0 = pl.multiple_of(h * P_TOPK, P_TOPK)
        e_ref[half, pl.ds(r0, P_TOPK), :] = eidx
        g_ref[half, pl.ds(r0, P_TOPK), :] = gate

    lax.fori_loop(0, P_HEADS, one_head, 0)


def _route(h1, wq_t, sk, tm):
    m, d = h1.shape
    n_half = tm // LANES
    nk = P_HEADS * P_TOPK
    kern = functools.partial(_route_kernel, n_half=n_half)
    return pl.pallas_call(
        kern,
        out_shape=(jax.ShapeDtypeStruct((m // LANES, nk, LANES), I32),
                   jax.ShapeDtypeStruct((m // LANES, nk, LANES), F32)),
        grid=(m // tm,),
        in_specs=[pl.BlockSpec((tm, d), lambda i: (i, 0)),
                  pl.BlockSpec(wq_t.shape, lambda i: (0, 0)),
                  pl.BlockSpec(sk.shape, lambda i: (0, 0, 0, 0))],
        out_specs=(pl.BlockSpec((n_half, nk, LANES), lambda i: (i, 0, 0)),
                   pl.BlockSpec((n_half, nk, LANES), lambda i: (i, 0, 0))),
        scratch_shapes=[pltpu.VMEM((n_half, wq_t.shape[0], LANES), BF16)],
        compiler_params=_cparams("parallel"),
        name="route",
    )(h1, wq_t, sk)


def _gelu_tanh(x):
    return 0.5 * x * (1.0 + jnp.tanh(math.sqrt(2.0 / math.pi) * (x + 0.044715 * (x * x * x))))


def _apply_kernel(idx_ref, idxn_ref, x_ref, gate_ref, g_ref, b_ref, uv_hbm, o_ref, *scratch,
                  tt, group, nk, dsub):
    buf = scratch[:tt]
    nbuf = 2 * group
    zs, cb, ys = (scratch[tt + s * nbuf:tt + (s + 1) * nbuf] for s in range(3))
    sem = scratch[tt + 3 * nbuf]
    i = pl.program_id(0)
    n = pl.num_programs(0)
    lane = lax.broadcasted_iota(I32, (SUBLANES, nk), 1)
    cnt = float(dsub * LANES)

    def issue(src_idx, t, slot):
        for k in range(nk):
            pltpu.make_async_copy(uv_hbm.at[src_idx[t, k]], buf[slot].at[k],
                                  sem.at[slot]).start(priority=k % 2)

    def wait(slot):
        pltpu.make_async_copy(uv_hbm.at[pl.ds(0, nk)], buf[slot], sem.at[slot]).wait()

    def dots(j):
        x = x_ref[j]
        parts = [jnp.zeros((SUBLANES, nk), F32) for _ in range(4)]
        for k in range(nk):
            p = buf[j][k, 0:dsub, :].astype(F32) * x
            r = p[0:SUBLANES]
            for c in range(1, dsub // SUBLANES):
                r = r + p[c * SUBLANES:(c + 1) * SUBLANES]
            parts[k % 4] = jnp.where(lane == k, jnp.sum(r, axis=1, keepdims=True), parts[k % 4])
        zs[j % nbuf][...] = (parts[0] + parts[1]) + (parts[2] + parts[3])

    def coefs(j):
        act = jnp.sum(zs[j % nbuf][...], axis=0, keepdims=True)
        coef = _gelu_tanh(act) * gate_ref[j:j + 1, :]
        cb[j % nbuf][...] = jnp.transpose(jnp.broadcast_to(coef, (nk, nk)))

    def mix(j):
        accs = [None] * 8
        for k in range(nk):
            term = buf[j][k, dsub:2 * dsub, :].astype(F32) * cb[j % nbuf][k:k + 1, :]
            accs[k % 8] = term if accs[k % 8] is None else accs[k % 8] + term
        ys[j % nbuf][...] = ((accs[0] + accs[1]) + (accs[2] + accs[3])) + ((accs[4] + accs[5]) + (accs[6] + accs[7]))

    def norm(j):
        r = ALPHA * x_ref[j] + ys[j % nbuf][...]
        mu = jnp.sum(jnp.sum(r, axis=1, keepdims=True), axis=0, keepdims=True) / cnt
        rc = r - mu
        var = jnp.sum(jnp.sum(rc * rc, axis=1, keepdims=True), axis=0, keepdims=True) / cnt
        o_ref[j] = rc * lax.rsqrt(var + EPS) * g_ref[...] + b_ref[...]

    depth = 3
    ng = tt // group
    toks = lambda g: range(g * group, (g + 1) * group)

    @pl.when(i == 0)
    def _prologue():
        for j in range(tt - depth * group):
            issue(idx_ref, j, j)

    def body(last):
        for m in range(ng + depth):
            if m < ng:
                for j in toks(m):
                    wait(j)
                for j in toks(m):
                    if m < depth:
                        issue(idx_ref, j + tt - depth * group, j + tt - depth * group)
                    elif not last:
                        issue(idxn_ref, j - depth * group, j - depth * group)
                for j in toks(m):
                    dots(j)
            for stage, fn in ((1, coefs), (2, mix), (3, norm)):
                if 0 <= m - stage < ng:
                    for j in toks(m - stage):
                        fn(j)

    @pl.when(i + 1 < n)
    def _steady():
        body(False)

    @pl.when(i + 1 == n)
    def _final():
        body(True)


def _apply(idx, h1_3, gates, g3, b3, uv3, tt, group):
    m, nk = idx.shape
    dsub = h1_3.shape[1]
    n_steps = m // tt
    nbuf = 2 * group
    kern = functools.partial(_apply_kernel, tt=tt, group=group, nk=nk, dsub=dsub)
    return pl.pallas_call(
        kern,
        out_shape=jax.ShapeDtypeStruct(h1_3.shape, F32),
        grid=(n_steps,),
        in_specs=[pl.BlockSpec((tt, nk), lambda i: (i, 0), memory_space=pltpu.SMEM),
                  pl.BlockSpec((tt, nk), lambda i: (jnp.minimum(i + 1, n_steps - 1), 0),
                               memory_space=pltpu.SMEM),
                  pl.BlockSpec((tt, dsub, LANES), lambda i: (i, 0, 0)),
                  pl.BlockSpec((tt, nk), lambda i: (i, 0)),
                  pl.BlockSpec((dsub, LANES), lambda i: (0, 0)),
                  pl.BlockSpec((dsub, LANES), lambda i: (0, 0)),
                  pl.BlockSpec(memory_space=pl.ANY)],
        out_specs=pl.BlockSpec((tt, dsub, LANES), lambda i: (i, 0, 0)),
        scratch_shapes=[pltpu.VMEM((nk, 2 * dsub, LANES), BF16) for _ in range(tt)]
        + [pltpu.VMEM((SUBLANES, nk), F32)] * nbuf + [pltpu.VMEM((nk, LANES), F32)] * nbuf
        + [pltpu.VMEM((dsub, LANES), F32)] * nbuf + [pltpu.SemaphoreType.DMA((tt,))],
        compiler_params=_cparams("arbitrary"),
        name="peer_apply",
    )(idx, idx, h1_3, gates, g3, b3, uv3)


def _regroup_w_in(w):
    d = w.shape[0]
    widths = (A_HEADS * A_HEAD_DIM, A_KV_RANK, IDX_HEADS * IDX_DIM, IDX_DIM, IDX_HEADS,
              G_HEADS * G_KEY_DIM, G_HEADS * G_KEY_DIM, G_HEADS * G_VAL_DIM, G_GATE_RANK,
              G_HEADS * G_VAL_DIM)
    offs = np.concatenate([[0], np.cumsum(widths)])
    a_q, a_ckv, i_q, i_k, i_w, g_q, g_k, g_v, g_r, g_o = [w[:, offs[n]:offs[n + 1]] for n in range(10)]
    z = lambda n: jnp.zeros((d, n), w.dtype)
    cat = jnp.concatenate([a_q, i_q, g_v, g_o, g_q, g_k, a_ckv,
                           i_k, i_w, z(LANES - IDX_DIM - IDX_HEADS),
                           g_r, z(LANES - G_GATE_RANK)], axis=1)
    assert cat.shape[1] == N_COLS
    return cat.astype(BF16)


def _dsa_groups(n_qblocks, n_groups):
    per = -(-n_qblocks // n_groups)
    return [(lo, min(per, n_qblocks - lo)) for lo in range(0, n_qblocks, per)]


def kernel(x, meta_tokens, ln0_g, ln0_b, rel_bias, w_in, w_uk, w_uv, w_gk2, b_gk, gla_norm_g, w_out,
           ln1_g, ln1_b, w_pq, sub_keys, u_tab, v_tab, ln2_g, ln2_b):
    bsz, s, d = x.shape
    t = bsz * s
    topk = min(TOPK_MAX, s // 4)
    assert s % Q_BLOCK == 0 and d % (SUBLANES * LANES) == 0
    _check_bucket_saturation(s + N_META)
    row2 = lambda v: v.reshape(1, -1)

    wcat = _regroup_w_in(w_in[0])
    h, proj = _ln_proj(x.reshape(t, d), row2(ln0_g), row2(ln0_b), wcat, tm=1024, tn=512)
    _, proj_meta = _ln_proj(meta_tokens, row2(ln0_g), row2(ln0_b), wcat, tm=N_META, tn=512)
    proj3 = proj.reshape(bsz, s, N_COLS)

    pad_rows = lambda a: jnp.pad(a, ((0, LANES - N_META), (0, 0)))
    ckv_meta = pad_rows(proj_meta[:, C_CKV:C_CKV + A_KV_RANK])
    idx_meta = pad_rows(proj_meta[:, C_IDX:C_IDX + LANES])
    btab = _bias_tab(rel_bias)
    wuk = w_uk[0].astype(BF16)
    wuv = w_uv[0].astype(BF16)
    ya_parts = []
    for q_lo, n_q in _dsa_groups(s // Q_BLOCK, 8):
        width = (q_lo + n_q) * Q_BLOCK
        ya_parts.append(_dsa(proj3, ckv_meta, idx_meta, wuk, wuv, btab, q_lo, n_q, width, topk))
    y_a = jnp.concatenate(ya_parts, axis=1).reshape(t, A_HEADS * A_HEAD_DIM)

    y_b = _gla(proj3, proj_meta, w_gk2[0], row2(b_gk[0]), row2(gla_norm_g[0])).reshape(t, -1)

    wo = w_out[0].astype(BF16)
    ka = A_HEADS * A_HEAD_DIM
    h1 = _out_ln(y_a, y_b, h, wo[:ka], wo[ka:], row2(ln1_g[0]), row2(ln1_b[0]), tm=256)

    wq_t = w_pq[0].T.astype(BF16)
    eidx, gates = _route(h1, wq_t, sub_keys[0].astype(BF16), tm=256)
    nk = P_HEADS * P_TOPK
    idx = eidx.transpose(0, 2, 1).reshape(t, nk)
    gates = gates.transpose(0, 2, 1).reshape(t, nk)
    dsub = d // LANES
    uv3 = jnp.concatenate([u_tab[0].reshape(-1, dsub, LANES), v_tab[0].reshape(-1, dsub, LANES)],
                          axis=1).astype(BF16)
    out3 = _apply(idx, h1.reshape(t, dsub, LANES), gates, ln2_g[0].reshape(dsub, LANES),
                  ln2_b[0].reshape(dsub, LANES), uv3, tt=16, group=2)
    return out3.reshape(bsz, s, d)
```

```python
import functools
import math

import numpy as np
import jax
import jax.numpy as jnp
from jax import lax
from jax.experimental import pallas as pl
from jax.experimental.pallas import tpu as pltpu

F32 = jnp.float32
BF16 = jnp.bfloat16
I32 = jnp.int32

N_META = 16
CHUNK = 64
SUB_BLOCK = 16
EPS = 1e-5
DEPTH = 1
ALPHA = (2 * DEPTH) ** 0.25
A_HEADS = 8
A_HEAD_DIM = 128
A_KV_RANK = 256
IDX_HEADS = 16
IDX_DIM = 64
TOPK_MAX = 256
Q_BLOCK = 128
REL_BUCKETS = 32
REL_MAX_DIST = 128
G_HEADS = 4
G_VAL_DIM = 256
G_KEY_DIM = 128
G_GATE_RANK = 16
G_GATE_NORM = 16.0
P_HEADS = 8
P_NKEYS = 128
P_QDIM = 256
P_TOPK = 16

LANES = 128
SUBLANES = 8
VMEM_LIMIT = 56 * 1024 * 1024

C_AQ, C_IQ, C_GV, C_GO = 0, 1024, 2048, 3072
C_GQ, C_GK = 4096, 4608
C_CKV = 5120
C_IDX = 5376
C_GR = 5504
N_COLS = 5632

INT_MIN = -2 ** 31
NT_DIMS = (((1,), (1,)), ((), ()))
TN_DIMS = (((0,), (0,)), ((), ()))


def _cparams(*sem):
    return pltpu.CompilerParams(dimension_semantics=sem, vmem_limit_bytes=VMEM_LIMIT)


def _layer_norm(x, g, b):
    mu = jnp.mean(x, axis=-1, keepdims=True)
    xc = x - mu
    var = jnp.mean(xc * xc, axis=-1, keepdims=True)
    return xc * lax.rsqrt(var + EPS) * g + b


def _ln_proj_kernel(x_ref, g_ref, b_ref, w_ref, h_ref, p_ref, hb_ref):
    @pl.when(pl.program_id(1) == 0)
    def _():
        y = _layer_norm(x_ref[...], g_ref[...], b_ref[...])
        h_ref[...] = y
        hb_ref[...] = y.astype(BF16)

    p_ref[...] = jnp.dot(hb_ref[...], w_ref[...], preferred_element_type=F32)


def _ln_proj(x2, g, b, wcat, tm, tn):
    m, d = x2.shape
    n = wcat.shape[1]
    return pl.pallas_call(
        _ln_proj_kernel,
        out_shape=(jax.ShapeDtypeStruct((m, d), F32), jax.ShapeDtypeStruct((m, n), F32)),
        grid=(m // tm, n // tn),
        in_specs=[pl.BlockSpec((tm, d), lambda i, j: (i, 0)),
                  pl.BlockSpec((1, d), lambda i, j: (0, 0)),
                  pl.BlockSpec((1, d), lambda i, j: (0, 0)),
                  pl.BlockSpec((d, tn), lambda i, j: (0, j))],
        out_specs=(pl.BlockSpec((tm, d), lambda i, j: (i, 0)),
                   pl.BlockSpec((tm, tn), lambda i, j: (i, j))),
        scratch_shapes=[pltpu.VMEM((tm, d), BF16)],
        compiler_params=_cparams("parallel", "arbitrary"),
        name="ln_proj",
    )(x2, g, b, wcat)


N_PAT = 5


def _t5_bucket(rel):
    nb = REL_BUCKETS // 2
    max_exact = nb // 2
    base = jnp.where(rel > 0, nb, 0)
    n = jnp.abs(rel)
    nf = jnp.maximum(n, 1).astype(jnp.float32)
    large = max_exact + (jnp.log(nf / max_exact) / math.log(REL_MAX_DIST / max_exact)
                         * (nb - max_exact)).astype(jnp.int32)
    large = jnp.minimum(large, nb - 1)
    return base + jnp.where(n < max_exact, n, large)


def _bias_patterns():
    i = jnp.arange(Q_BLOCK, dtype=I32)[:, None]
    j = jnp.arange(LANES, dtype=I32)[None, :]
    rels = [j - i, j - i - LANES, j - i - 2 * LANES, j - N_META - i, j - N_META - i - LANES]
    return jnp.stack([_t5_bucket(r) for r in rels])


def _check_bucket_saturation(max_dist):
    nb = REL_BUCKETS // 2
    max_exact = nb // 2
    n = np.arange(LANES + 1, max_dist + 1, dtype=np.float64)
    large = max_exact + np.floor(np.log(n / max_exact) / math.log(REL_MAX_DIST / max_exact) * (nb - max_exact))
    assert np.all(large >= nb - 1 + 0.5), "relative-position buckets do not saturate within one key tile"


def _bias_tab_kernel(bk_ref, rb_ref, o_ref):
    h = pl.program_id(0)
    for p in range(N_PAT):
        bk = bk_ref[p]
        acc = jnp.zeros((Q_BLOCK, LANES), F32)
        for b in range(REL_BUCKETS):
            acc = jnp.where(bk == b, rb_ref[b, h], acc)
        o_ref[0, p] = acc


def _bias_tab(rel_bias):
    return pl.pallas_call(
        _bias_tab_kernel,
        out_shape=jax.ShapeDtypeStruct((A_HEADS, N_PAT, Q_BLOCK, LANES), F32),
        grid=(A_HEADS,),
        in_specs=[pl.BlockSpec((N_PAT, Q_BLOCK, LANES), lambda h: (0, 0, 0)),
                  pl.BlockSpec(memory_space=pltpu.SMEM)],
        out_specs=pl.BlockSpec((1, N_PAT, Q_BLOCK, LANES), lambda h: (h, 0, 0, 0)),
        compiler_params=_cparams("arbitrary"),
        name="bias_tab",
    )(_bias_patterns(), rel_bias)


def _dsa_kernel(aq_ref, iq_ref, idxq_ref, ckv_ref, idxk_ref, ckvm_ref, idxm_ref, wuk_ref, wuv_ref,
                bt_ref, o_ref,
                ik_scr, ckvb_scr, iqs_scr, l_scr, key_scr, el_scr, madd_scr, ql_scr, lg_scr, p_scr,
                den_scr, *, q_lo, width, topk):
    nt = 1 + width // LANES
    wt = nt * LANES
    qi = pl.program_id(1) + q_lo

    @pl.when(pl.program_id(1) == 0)
    def _stage_keys():
        ik_scr[0:LANES, :] = idxm_ref[:, 0:IDX_DIM].astype(BF16)
        ik_scr[LANES:wt, :] = idxk_ref[0, :, 0:IDX_DIM].astype(BF16)
        ckvb_scr[0:LANES, :] = ckvm_ref[...].astype(BF16)
        ckvb_scr[LANES:wt, :] = ckv_ref[0].astype(BF16)

    row = lax.broadcasted_iota(I32, (Q_BLOCK, LANES), 0)
    lane = lax.broadcasted_iota(I32, (Q_BLOCK, LANES), 1)
    kb = ((qi * Q_BLOCK + row) // CHUNK + 1) * CHUNK

    iq = iq_ref[0].astype(BF16)
    for h in range(IDX_HEADS):
        iqs_scr[h * Q_BLOCK:(h + 1) * Q_BLOCK, :] = iq[:, h * IDX_DIM:(h + 1) * IDX_DIM]
    iw = idxq_ref[0][:, IDX_DIM:IDX_DIM + IDX_HEADS] * (IDX_HEADS ** -0.5 * IDX_DIM ** -0.5)
    iw_b = [jnp.broadcast_to(iw[:, h:h + 1], (Q_BLOCK, LANES)) for h in range(IDX_HEADS)]
    chunks = [(c0, min(2 * LANES, wt - c0)) for c0 in range(0, wt, 2 * LANES)]
    for ci, (c0, cw) in enumerate(chunks):
        l_buf = l_scr.at[ci % 2]
        l_buf[:, 0:cw] = lax.dot_general(iqs_scr[...], ik_scr[c0:c0 + cw, :], NT_DIMS,
                                         preferred_element_type=F32)
        for kt in range(c0 // LANES, (c0 + cw) // LANES):
            cols = slice(kt * LANES - c0, (kt + 1) * LANES - c0)
            acc = jnp.zeros((Q_BLOCK, LANES), F32)
            for h in range(IDX_HEADS):
                acc = acc + jnp.maximum(l_buf[h * Q_BLOCK:(h + 1) * Q_BLOCK, cols], 0.0) * iw_b[h]
            adm = (lane < N_META) if kt == 0 else (lane + (kt - 1) * LANES < kb)
            bits = pltpu.bitcast(acc + 0.0, I32)
            key = jnp.where(bits < 0, bits ^ 0x7FFFFFFF, bits)
            key_scr[:, kt * LANES:(kt + 1) * LANES] = jnp.where(adm, key, INT_MIN)

    def count(pred):
        acc = jnp.zeros((Q_BLOCK, LANES), F32)
        for kt in range(nt):
            acc = acc + jnp.where(pred(kt), 1.0, 0.0)
        return jnp.sum(acc, axis=1, keepdims=True)

    def key_tile(kt):
        return key_scr[:, kt * LANES:(kt + 1) * LANES]

    def val_step(it, thr):
        cand = thr + jnp.left_shift(jnp.int32(1), 31 - it)
        cnt = count(lambda kt: key_tile(kt) >= cand)
        return jnp.where(cnt >= topk, cand, thr)

    thr = lax.fori_loop(0, 32, val_step, jnp.full((Q_BLOCK, 1), INT_MIN, I32))
    n_ge = count(lambda kt: key_tile(kt) >= thr)
    thr_adm = jnp.maximum(thr, INT_MIN + 1)
    for kt in range(nt):
        madd_scr[:, kt * LANES:(kt + 1) * LANES] = jnp.where(key_tile(kt) >= thr_adm, 0.0, -jnp.inf)
    tied = jnp.where((n_ge > topk) & (thr > INT_MIN), 1.0, 0.0)

    @pl.when(jnp.max(tied) > 0.0)
    def _break_ties():
        need = topk - count(lambda kt: key_tile(kt) > thr)
        big = jnp.int32(2 ** 30)
        for kt in range(nt):
            el_scr[:, kt * LANES:(kt + 1) * LANES] = jnp.where(key_tile(kt) == thr, lane + kt * LANES, big)

        def el_tile(kt):
            return el_scr[:, kt * LANES:(kt + 1) * LANES]

        nbits = max(1, int(wt - 1).bit_length())

        def pos_step(it, x):
            cand = x + jnp.left_shift(jnp.int32(1), nbits - 1 - it)
            cnt = count(lambda kt: el_tile(kt) < cand)
            return jnp.where(cnt < need, cand, x)

        xpos = lax.fori_loop(0, nbits, pos_step, jnp.zeros((Q_BLOCK, 1), I32))
        for kt in range(nt):
            sel = ((key_tile(kt) > thr) | (el_tile(kt) <= xpos)) & (key_tile(kt) > INT_MIN)
            madd_scr[:, kt * LANES:(kt + 1) * LANES] = jnp.where(sel, 0.0, -jnp.inf)

    aq = aq_ref[0].astype(BF16)
    for h in range(A_HEADS):
        qh = lax.dot_general(aq[:, h * A_HEAD_DIM:(h + 1) * A_HEAD_DIM], wuk_ref[h], NT_DIMS,
                             preferred_element_type=F32) * (A_HEAD_DIM ** -0.5)
        ql_scr[h * Q_BLOCK:(h + 1) * Q_BLOCK, :] = qh.astype(BF16)
    for c0, cw in chunks:
        lg_scr[:, c0:c0 + cw] = lax.dot_general(
            ql_scr[...], ckvb_scr[c0:c0 + cw, :], NT_DIMS, preferred_element_type=F32)

    def head_body(h, carry):
        r0 = pl.multiple_of(h * Q_BLOCK, Q_BLOCK)
        rows = pl.ds(r0, Q_BLOCK)
        m = jnp.full((Q_BLOCK, LANES), -jnp.inf, F32)
        for kt in range(nt):
            cols = slice(kt * LANES, (kt + 1) * LANES)
            pat = (3 + jnp.minimum(qi, 1)) if kt == 0 else jnp.clip(qi - (kt - 1), 0, 2)
            l = lg_scr[rows, cols] + bt_ref[h, pat] + madd_scr[:, cols]
            lg_scr[rows, cols] = l
            m = jnp.maximum(m, l)
        mrow = jnp.max(m, axis=1, keepdims=True)
        ssum = jnp.zeros((Q_BLOCK, LANES), F32)
        for kt in range(nt):
            cols = slice(kt * LANES, (kt + 1) * LANES)
            e = jnp.exp(lg_scr[rows, cols] - mrow)
            ssum = ssum + e
            p_scr[rows, cols] = e.astype(BF16)
        den_scr[rows, :] = jnp.broadcast_to(jnp.sum(ssum, axis=1, keepdims=True), (Q_BLOCK, LANES))
        return carry

    lax.fori_loop(0, A_HEADS, head_body, 0)
    o = jnp.dot(p_scr[...], ckvb_scr[...], preferred_element_type=F32)
    for h in range(A_HEADS):
        rs = slice(h * Q_BLOCK, (h + 1) * Q_BLOCK)
        oh = o[rs] / den_scr[rs, 0:1]
        o_ref[0, :, h * A_HEAD_DIM:(h + 1) * A_HEAD_DIM] = jnp.dot(
            oh.astype(BF16), wuv_ref[h], preferred_element_type=F32)


def _dsa(proj3, ckv_meta, idx_meta, wuk, wuv, btab, q_lo, n_q, width, topk):
    bsz, s, _ = proj3.shape
    nt = 1 + width // LANES
    wt = nt * LANES
    hq = A_HEADS * Q_BLOCK
    kern = functools.partial(_dsa_kernel, q_lo=q_lo, width=width, topk=topk)
    return pl.pallas_call(
        kern,
        out_shape=jax.ShapeDtypeStruct((bsz, n_q * Q_BLOCK, A_HEADS * A_HEAD_DIM), F32),
        grid=(bsz, n_q),
        in_specs=[
            pl.BlockSpec((1, Q_BLOCK, 1024), lambda b, q: (b, q + q_lo, C_AQ // 1024)),
            pl.BlockSpec((1, Q_BLOCK, 1024), lambda b, q: (b, q + q_lo, C_IQ // 1024)),
            pl.BlockSpec((1, Q_BLOCK, LANES), lambda b, q: (b, q + q_lo, C_IDX // LANES)),
            pl.BlockSpec((1, width, A_KV_RANK), lambda b, q: (b, 0, C_CKV // A_KV_RANK)),
            pl.BlockSpec((1, width, LANES), lambda b, q: (b, 0, C_IDX // LANES)),
            pl.BlockSpec((LANES, A_KV_RANK), lambda b, q: (0, 0)),
            pl.BlockSpec((LANES, LANES), lambda b, q: (0, 0)),
            pl.BlockSpec((A_HEADS, A_KV_RANK, A_HEAD_DIM), lambda b, q: (0, 0, 0)),
            pl.BlockSpec((A_HEADS, A_KV_RANK, A_HEAD_DIM), lambda b, q: (0, 0, 0)),
            pl.BlockSpec((A_HEADS, N_PAT, Q_BLOCK, LANES), lambda b, q: (0, 0, 0, 0)),
        ],
        out_specs=pl.BlockSpec((1, Q_BLOCK, A_HEADS * A_HEAD_DIM), lambda b, q: (b, q, 0)),
        scratch_shapes=[
            pltpu.VMEM((wt, IDX_DIM), BF16),
            pltpu.VMEM((wt, A_KV_RANK), BF16),
            pltpu.VMEM((IDX_HEADS * Q_BLOCK, IDX_DIM), BF16),
            pltpu.VMEM((2, IDX_HEADS * Q_BLOCK, 2 * LANES), F32),
            pltpu.VMEM((Q_BLOCK, wt), I32),
            pltpu.VMEM((Q_BLOCK, wt), I32),
            pltpu.VMEM((Q_BLOCK, wt), F32),
            pltpu.VMEM((hq, A_KV_RANK), BF16),
            pltpu.VMEM((hq, wt), F32),
            pltpu.VMEM((hq, wt), BF16),
            pltpu.VMEM((hq, LANES), F32),
        ],
        compiler_params=_cparams("parallel", "arbitrary"),
        name=f"dsa_w{width}",
    )(proj3, proj3, proj3, proj3, proj3, ckv_meta, idx_meta, wuk, wuv, btab)


def _gla_kernel(gq_ref, gk_ref, gv_ref, go_ref, gr_ref, gkm_ref, gvm_ref, grm_ref, w2_ref, bg_ref,
                ng_ref, o_ref, st_scr):
    w2 = w2_ref[...].astype(BF16)

    def log_decay(gr):
        z = jnp.dot(gr.astype(BF16), w2, preferred_element_type=F32) + bg_ref[...]
        return (jnp.minimum(z, 0.0) - jnp.log(1.0 + jnp.exp(-jnp.abs(z)))) * (1.0 / G_GATE_NORM)

    def cumsum_rows(g):
        n = g.shape[0]
        tri = (lax.broadcasted_iota(I32, (n, n), 0) >= lax.broadcasted_iota(I32, (n, n), 1)).astype(F32)
        return jnp.dot(tri, g, precision=lax.Precision.HIGHEST, preferred_element_type=F32)

    @pl.when(pl.program_id(1) == 0)
    def _init_state():
        bm = cumsum_rows(log_decay(grm_ref[:, 0:G_GATE_RANK]))
        kd = gkm_ref[...] * jnp.exp(bm[N_META - 1:N_META, :] - bm)
        for h in range(G_HEADS):
            st_scr[h] = lax.dot_general(
                gvm_ref[:, h * G_VAL_DIM:(h + 1) * G_VAL_DIM].astype(BF16),
                kd[:, h * G_KEY_DIM:(h + 1) * G_KEY_DIM].astype(BF16), TN_DIMS,
                preferred_element_type=F32)

    b = cumsum_rows(log_decay(gr_ref[0][:, 0:G_GATE_RANK]))
    row = lax.broadcasted_iota(I32, (CHUNK, CHUNK), 0)
    col = lax.broadcasted_iota(I32, (CHUNK, CHUNK), 1)
    col_sb = lax.broadcasted_iota(I32, (SUB_BLOCK, CHUNK), 1)
    for h in range(G_HEADS):
        ks = slice(h * G_KEY_DIM, (h + 1) * G_KEY_DIM)
        vs = slice(h * G_VAL_DIM, (h + 1) * G_VAL_DIM)
        q = gq_ref[0][:, ks] * (G_KEY_DIM ** -0.5)
        k = gk_ref[0][:, ks]
        v = gv_ref[0][:, vs].astype(BF16)
        bh = b[:, ks]
        st = st_scr[h]
        o = lax.dot_general((q * jnp.exp(bh)).astype(BF16), st.astype(BF16), NT_DIMS,
                            preferred_element_type=F32)
        blocks = []
        for ib in range(CHUNK // SUB_BLOCK):
            rs = slice(ib * SUB_BLOCK, (ib + 1) * SUB_BLOCK)
            qi, bi = q[rs], bh[rs]
            ai = jnp.zeros((SUB_BLOCK, CHUNK), F32)
            if ib > 0:
                br = bi[0:1, :]
                kt = k * jnp.exp(jnp.minimum(br - bh, 0.0))
                ai = lax.dot_general(qi * jnp.exp(bi - br), kt, NT_DIMS, precision=lax.Precision.HIGHEST,
                                     preferred_element_type=F32)
                ai = jnp.where(col_sb < ib * SUB_BLOCK, ai, 0.0)
            for j in range(ib * SUB_BLOCK, (ib + 1) * SUB_BLOCK):
                w = qi * k[j:j + 1, :] * jnp.exp(jnp.minimum(bi - bh[j:j + 1, :], 0.0))
                ai = jnp.where(col_sb == j, jnp.sum(w, axis=1, keepdims=True), ai)
            blocks.append(ai)
        a = jnp.concatenate(blocks, axis=0)
        a = jnp.where(row >= col, a, 0.0)
        o = o + jnp.dot(a.astype(BF16), v, preferred_element_type=F32)
        bl = bh[CHUNK - 1:CHUNK, :]
        kd = k * jnp.exp(bl - bh)
        st_scr[h] = st * jnp.exp(bl) + lax.dot_general(v, kd.astype(BF16), TN_DIMS,
                                                       preferred_element_type=F32)
        on = o * lax.rsqrt(jnp.mean(o * o, axis=1, keepdims=True) + EPS) * ng_ref[...]
        og = go_ref[0][:, vs]
        o_ref[0, :, vs] = on * (og * (1.0 / (1.0 + jnp.exp(-og))))


def _gla(proj3, proj_meta, w_gk2, b_gk, norm_g):
    bsz, s, _ = proj3.shape
    hk = G_HEADS * G_KEY_DIM
    hv = G_HEADS * G_VAL_DIM
    gkm = proj_meta[:, C_GK:C_GK + hk]
    gvm = proj_meta[:, C_GV:C_GV + hv]
    grm = proj_meta[:, C_GR:C_GR + LANES]
    full = lambda shp: pl.BlockSpec(shp, lambda b, c: (0,) * len(shp))
    return pl.pallas_call(
        _gla_kernel,
        out_shape=jax.ShapeDtypeStruct((bsz, s, hv), F32),
        grid=(bsz, s // CHUNK),
        in_specs=[
            pl.BlockSpec((1, CHUNK, hk), lambda b, c: (b, c, C_GQ // hk)),
            pl.BlockSpec((1, CHUNK, hk), lambda b, c: (b, c, C_GK // hk)),
            pl.BlockSpec((1, CHUNK, hv), lambda b, c: (b, c, C_GV // hv)),
            pl.BlockSpec((1, CHUNK, hv), lambda b, c: (b, c, C_GO // hv)),
            pl.BlockSpec((1, CHUNK, LANES), lambda b, c: (b, c, C_GR // LANES)),
            full((N_META, hk)), full((N_META, hv)), full((N_META, LANES)),
            full((G_GATE_RANK, hk)), full((1, hk)), full((1, G_VAL_DIM)),
        ],
        out_specs=pl.BlockSpec((1, CHUNK, hv), lambda b, c: (b, c, 0)),
        scratch_shapes=[pltpu.VMEM((G_HEADS, G_VAL_DIM, G_KEY_DIM), F32)],
        compiler_params=_cparams("parallel", "arbitrary"),
        name="gla",
    )(proj3, proj3, proj3, proj3, proj3, gkm, gvm, grm, w_gk2, b_gk, norm_g)


def _out_ln_kernel(ya_ref, yb_ref, h_ref, wa_ref, wb_ref, g_ref, b_ref, o_ref):
    mix = (jnp.dot(ya_ref[...].astype(BF16), wa_ref[...], preferred_element_type=F32)
           + jnp.dot(yb_ref[...].astype(BF16), wb_ref[...], preferred_element_type=F32))
    o_ref[...] = _layer_norm(ALPHA * h_ref[...] + mix, g_ref[...], b_ref[...])


def _out_ln(ya, yb, h, wa, wb, g, b, tm):
    m, d = h.shape
    ka, kb = ya.shape[1], yb.shape[1]
    return pl.pallas_call(
        _out_ln_kernel,
        out_shape=jax.ShapeDtypeStruct((m, d), F32),
        grid=(m // tm,),
        in_specs=[pl.BlockSpec((tm, ka), lambda i: (i, 0)),
                  pl.BlockSpec((tm, kb), lambda i: (i, 0)),
                  pl.BlockSpec((tm, d), lambda i: (i, 0)),
                  pl.BlockSpec((ka, d), lambda i: (0, 0)),
                  pl.BlockSpec((kb, d), lambda i: (0, 0)),
                  pl.BlockSpec((1, d), lambda i: (0, 0)),
                  pl.BlockSpec((1, d), lambda i: (0, 0))],
        out_specs=pl.BlockSpec((tm, d), lambda i: (i, 0)),
        compiler_params=_cparams("parallel"),
        name="out_ln",
    )(ya, yb, h, wa, wb, g, b)


RANK_NONE = 2 ** 20


def _top_rows(s, rank, k):
    slot = lax.broadcasted_iota(I32, (k, LANES), 0)
    vals = jnp.zeros((k, LANES), F32)
    idxs = jnp.zeros((k, LANES), I32)
    for r in range(k):
        m = jnp.max(s, axis=0, keepdims=True)
        im = jnp.min(jnp.where(s == m, rank, RANK_NONE), axis=0, keepdims=True)
        vals = jnp.where(slot == r, m, vals)
        idxs = jnp.where(slot == r, im, idxs)
        s = jnp.where(rank == im, -jnp.inf, s)
    return vals, idxs


def _product_candidates(s1, i1, s2, i2):
    sub = lax.broadcasted_iota(I32, (SUBLANES, LANES), 0)
    lo = sub < 4
    s2a, s2b = s2[0:SUBLANES], s2[SUBLANES:2 * SUBLANES]
    i2a, i2b = i2[0:SUBLANES], i2[SUBLANES:2 * SUBLANES]
    s2d = jnp.where(lo, s2a, pltpu.roll(s2a, 4, 0))
    i2d = jnp.where(lo, i2a, pltpu.roll(i2a, 4, 0))
    row = lambda v, a: jnp.broadcast_to(v[a:a + 1, :], (SUBLANES, LANES))
    pair = lambda v, a: jnp.where(lo, row(v, a), row(v, a + 1))
    e1 = i1 * P_NKEYS
    groups = [
        (row(s1, 0) + s2a, row(e1, 0) + i2a, sub, None),
        (row(s1, 0) + s2b, row(e1, 0) + i2b, sub + SUBLANES, None),
        (row(s1, 1) + s2a, row(e1, 1) + i2a, sub + P_TOPK, None),
        (row(s1, 2) + s2a, row(e1, 2) + i2a, sub + 2 * P_TOPK, sub < 5),
        (row(s1, 3) + s2a, row(e1, 3) + i2a, sub + 3 * P_TOPK, sub < 4),
        (pair(s1, 4) + s2d, pair(e1, 4) + i2d, jnp.where(lo, sub + 4 * P_TOPK, sub - 4 + 5 * P_TOPK),
         (sub < 3) | ((sub >= 4) & (sub < 6))),
        (pair(s1, 6) + s2d, pair(e1, 6) + i2d, jnp.where(lo, sub + 6 * P_TOPK, sub - 4 + 7 * P_TOPK),
         (sub < 2) | ((sub >= 4) & (sub < 6))),
        (s1[SUBLANES:] + row(s2, 0), e1[SUBLANES:] + row(i2, 0), (sub + SUBLANES) * P_TOPK, None),
    ]
    vals, eids, ranks = [], [], []
    for v, e, rk, ok in groups:
        vals.append(v if ok is None else jnp.where(ok, v, -jnp.inf))
        ranks.append(rk if ok is None else jnp.where(ok, rk, RANK_NONE))
        eids.append(e)
    cat = lambda xs: jnp.concatenate(xs, axis=0)
    return cat(vals), cat(ranks), cat(eids)


def _route_kernel(h_ref, wq_ref, sk_ref, e_ref, g_ref, qt_scr, *, n_half):
    qt = lax.dot_general(wq_ref[...], h_ref[...].astype(BF16), NT_DIMS, preferred_element_type=F32)
    for half in range(n_half):
        qt_scr[half] = qt[:, half * LANES:(half + 1) * LANES].astype(BF16)
    key_id = lax.broadcasted_iota(I32, (P_NKEYS, LANES), 0)
    slot = lax.broadcasted_iota(I32, (P_TOPK, LANES), 0)
    half_dim = P_QDIM // 2

    def one_head(h, carry):
        for half in range(n_half):
            unit(h, half)
        return carry

    def unit(h, half):
        tops = []
        for c in range(2):
            d0 = pl.multiple_of((h * 2 + c) * half_dim, half_dim)
            s = jnp.dot(sk_ref[h, c], qt_scr[half, pl.ds(d0, half_dim), :], preferred_element_type=F32)
            tops.append(_top_rows(s, key_id, P_TOPK))
        (s1, i1), (s2, i2) = tops
        cand, crank, cidx = _product_candidates(s1, i1, s2, i2)
        top, pos = _top_rows(cand, crank, P_TOPK)
        eidx = jnp.zeros((P_TOPK, LANES), I32)
        for r in range(P_TOPK):
            er = jnp.max(jnp.where(crank == pos[r:r + 1, :], cidx, -1), axis=0, keepdims=True)
            eidx = jnp.where(slot == r, er, eidx)
        ex = jnp.exp(top - top[0:1, :])
        gate = ex / jnp.sum(ex, axis=0, keepdims=True)
        r0 = pl.multiple_of(h * P_TOPK, P_TOPK)
        e_ref[half, pl.ds(r0, P_TOPK), :] = eidx
        g_ref[half, pl.ds(r0, P_TOPK), :] = gate

    lax.fori_loop(0, P_HEADS, one_head, 0)


def _route(h1, wq_t, sk, tm):
    m, d = h1.shape
    n_half = tm // LANES
    nk = P_HEADS * P_TOPK
    kern = functools.partial(_route_kernel, n_half=n_half)
    return pl.pallas_call(
        kern,
        out_shape=(jax.ShapeDtypeStruct((m // LANES, nk, LANES), I32),
                   jax.ShapeDtypeStruct((m // LANES, nk, LANES), F32)),
        grid=(m // tm,),
        in_specs=[pl.BlockSpec((tm, d), lambda i: (i, 0)),
                  pl.BlockSpec(wq_t.shape, lambda i: (0, 0)),
                  pl.BlockSpec(sk.shape, lambda i: (0, 0, 0, 0))],
        out_specs=(pl.BlockSpec((n_half, nk, LANES), lambda i: (i, 0, 0)),
                   pl.BlockSpec((n_half, nk, LANES), lambda i: (i, 0, 0))),
        scratch_shapes=[pltpu.VMEM((n_half, wq_t.shape[0], LANES), BF16)],
        compiler_params=_cparams("parallel"),
        name="route",
    )(h1, wq_t, sk)


def _gelu_tanh(x):
    return 0.5 * x * (1.0 + jnp.tanh(math.sqrt(2.0 / math.pi) * (x + 0.044715 * (x * x * x))))


def _apply_kernel(idx_ref, idxn_ref, x_ref, gate_ref, g_ref, b_ref, uv_hbm, o_ref, *scratch,
                  tt, group, nk, dsub):
    buf = scratch[:tt]
    nbuf = 2 * group
    zs, cb, ys, xd, xn = (scratch[tt + s * nbuf:tt + (s + 1) * nbuf] for s in range(5))
    sem = scratch[tt + 5 * nbuf]

    def load_x(j, scr):
        for c in range(dsub):
            scr[c:c + 1, :] = x_ref[j:j + 1, c * LANES:(c + 1) * LANES]
        return scr[...]

    i = pl.program_id(0)
    n = pl.num_programs(0)
    lane = lax.broadcasted_iota(I32, (SUBLANES, nk), 1)
    cnt = float(dsub * LANES)

    def issue(src_idx, t, slot):
        for k in range(nk):
            pltpu.make_async_copy(uv_hbm.at[src_idx[t, k]], buf[slot].at[k],
                                  sem.at[slot]).start(priority=k % 2)

    def wait(slot):
        pltpu.make_async_copy(uv_hbm.at[pl.ds(0, nk)], buf[slot], sem.at[slot]).wait()

    def dots(j):
        x = load_x(j, xd[j % nbuf])
        parts = [jnp.zeros((SUBLANES, nk), F32) for _ in range(4)]
        for k in range(nk):
            p = buf[j][k, 0:dsub, :].astype(F32) * x
            r = p[0:SUBLANES]
            for c in range(1, dsub // SUBLANES):
                r = r + p[c * SUBLANES:(c + 1) * SUBLANES]
            parts[k % 4] = jnp.where(lane == k, jnp.sum(r, axis=1, keepdims=True), parts[k % 4])
        zs[j % nbuf][...] = (parts[0] + parts[1]) + (parts[2] + parts[3])

    def coefs(j):
        act = jnp.sum(zs[j % nbuf][...], axis=0, keepdims=True)
        coef = _gelu_tanh(act) * gate_ref[j:j + 1, :]
        cb[j % nbuf][...] = jnp.transpose(jnp.broadcast_to(coef, (nk, nk)))

    def mix(j):
        accs = [None] * 8
        for k in range(nk):
            term = buf[j][k, dsub:2 * dsub, :].astype(F32) * cb[j % nbuf][k:k + 1, :]
            accs[k % 8] = term if accs[k % 8] is None else accs[k % 8] + term
        ys[j % nbuf][...] = ((accs[0] + accs[1]) + (accs[2] + accs[3])) + ((accs[4] + accs[5]) + (accs[6] + accs[7]))

    def norm(j):
        r = ALPHA * load_x(j, xn[j % nbuf]) + ys[j % nbuf][...]
        mu = jnp.sum(jnp.sum(r, axis=1, keepdims=True), axis=0, keepdims=True) / cnt
        rc = r - mu
        var = jnp.sum(jnp.sum(rc * rc, axis=1, keepdims=True), axis=0, keepdims=True) / cnt
        y = rc * lax.rsqrt(var + EPS) * g_ref[...] + b_ref[...]
        for c in range(dsub):
            o_ref[j:j + 1, c * LANES:(c + 1) * LANES] = y[c:c + 1, :]

    depth = 3
    ng = tt // group
    toks = lambda g: range(g * group, (g + 1) * group)

    @pl.when(i == 0)
    def _prologue():
        for j in range(tt - depth * group):
            issue(idx_ref, j, j)

    def body(last):
        for m in range(ng + depth):
            if m < ng:
                for j in toks(m):
                    wait(j)
                for j in toks(m):
                    if m < depth:
                        issue(idx_ref, j + tt - depth * group, j + tt - depth * group)
                    elif not last:
                        issue(idxn_ref, j - depth * group, j - depth * group)
                for j in toks(m):
                    dots(j)
            for stage, fn in ((1, coefs), (2, mix), (3, norm)):
                if 0 <= m - stage < ng:
                    for j in toks(m - stage):
                        fn(j)

    @pl.when(i + 1 < n)
    def _steady():
        body(False)

    @pl.when(i + 1 == n)
    def _final():
        body(True)


def _apply(idx, h1, gates, g3, b3, uv3, tt, group):
    m, nk = idx.shape
    d = h1.shape[1]
    dsub = d // LANES
    n_steps = m // tt
    nbuf = 2 * group
    kern = functools.partial(_apply_kernel, tt=tt, group=group, nk=nk, dsub=dsub)
    return pl.pallas_call(
        kern,
        out_shape=jax.ShapeDtypeStruct(h1.shape, F32),
        grid=(n_steps,),
        in_specs=[pl.BlockSpec((tt, nk), lambda i: (i, 0), memory_space=pltpu.SMEM),
                  pl.BlockSpec((tt, nk), lambda i: (jnp.minimum(i + 1, n_steps - 1), 0),
                               memory_space=pltpu.SMEM),
                  pl.BlockSpec((tt, d), lambda i: (i, 0)),
                  pl.BlockSpec((tt, nk), lambda i: (i, 0)),
                  pl.BlockSpec((dsub, LANES), lambda i: (0, 0)),
                  pl.BlockSpec((dsub, LANES), lambda i: (0, 0)),
                  pl.BlockSpec(memory_space=pl.ANY)],
        out_specs=pl.BlockSpec((tt, d), lambda i: (i, 0)),
        scratch_shapes=[pltpu.VMEM((nk, 2 * dsub, LANES), BF16) for _ in range(tt)]
        + [pltpu.VMEM((SUBLANES, nk), F32)] * nbuf + [pltpu.VMEM((nk, LANES), F32)] * nbuf
        + [pltpu.VMEM((dsub, LANES), F32)] * (3 * nbuf) + [pltpu.SemaphoreType.DMA((tt,))],
        compiler_params=_cparams("arbitrary"),
        name="peer_apply",
    )(idx, idx, h1, gates, g3, b3, uv3)


def _regroup_w_in(w):
    d = w.shape[0]
    widths = (A_HEADS * A_HEAD_DIM, A_KV_RANK, IDX_HEADS * IDX_DIM, IDX_DIM, IDX_HEADS,
              G_HEADS * G_KEY_DIM, G_HEADS * G_KEY_DIM, G_HEADS * G_VAL_DIM, G_GATE_RANK,
              G_HEADS * G_VAL_DIM)
    offs = np.concatenate([[0], np.cumsum(widths)])
    a_q, a_ckv, i_q, i_k, i_w, g_q, g_k, g_v, g_r, g_o = [w[:, offs[n]:offs[n + 1]] for n in range(10)]
    z = lambda n: jnp.zeros((d, n), w.dtype)
    cat = jnp.concatenate([a_q, i_q, g_v, g_o, g_q, g_k, a_ckv,
                           i_k, i_w, z(LANES - IDX_DIM - IDX_HEADS),
                           g_r, z(LANES - G_GATE_RANK)], axis=1)
    assert cat.shape[1] == N_COLS
    return cat.astype(BF16)


def _dsa_groups(n_qblocks, n_groups):
    per = -(-n_qblocks // n_groups)
    return [(lo, min(per, n_qblocks - lo)) for lo in range(0, n_qblocks, per)]


def kernel(x, meta_tokens, ln0_g, ln0_b, rel_bias, w_in, w_uk, w_uv, w_gk2, b_gk, gla_norm_g, w_out,
           ln1_g, ln1_b, w_pq, sub_keys, u_tab, v_tab, ln2_g, ln2_b):
    bsz, s, d = x.shape
    t = bsz * s
    topk = min(TOPK_MAX, s // 4)
    assert s % Q_BLOCK == 0 and d % (SUBLANES * LANES) == 0
    _check_bucket_saturation(s + N_META)
    row2 = lambda v: v.reshape(1, -1)

    wcat = _regroup_w_in(w_in[0])
    h, proj = _ln_proj(x.reshape(t, d), row2(ln0_g), row2(ln0_b), wcat, tm=1024, tn=512)
    _, proj_meta = _ln_proj(meta_tokens, row2(ln0_g), row2(ln0_b), wcat, tm=N_META, tn=512)
    proj3 = proj.reshape(bsz, s, N_COLS)

    pad_rows = lambda a: jnp.pad(a, ((0, LANES - N_META), (0, 0)))
    ckv_meta = pad_rows(proj_meta[:, C_CKV:C_CKV + A_KV_RANK])
    idx_meta = pad_rows(proj_meta[:, C_IDX:C_IDX + LANES])
    btab = _bias_tab(rel_bias)
    wuk = w_uk[0].astype(BF16)
    wuv = w_uv[0].astype(BF16)
    ya_parts = []
    for q_lo, n_q in _dsa_groups(s // Q_BLOCK, 8):
        width = (q_lo + n_q) * Q_BLOCK
        ya_parts.append(_dsa(proj3, ckv_meta, idx_meta, wuk, wuv, btab, q_lo, n_q, width, topk))
    y_a = jnp.concatenate(ya_parts, axis=1).reshape(t, A_HEADS * A_HEAD_DIM)

    y_b = _gla(proj3, proj_meta, w_gk2[0], row2(b_gk[0]), row2(gla_norm_g[0])).reshape(t, -1)

    wo = w_out[0].astype(BF16)
    ka = A_HEADS * A_HEAD_DIM
    h1 = _out_ln(y_a, y_b, h, wo[:ka], wo[ka:], row2(ln1_g[0]), row2(ln1_b[0]), tm=256)

    wq_t = w_pq[0].T.astype(BF16)
    eidx, gates = _route(h1, wq_t, sub_keys[0].astype(BF16), tm=256)
    nk = P_HEADS * P_TOPK
    idx = eidx.transpose(0, 2, 1).reshape(t, nk)
    gates = gates.transpose(0, 2, 1).reshape(t, nk)
    dsub = d // LANES
    uv3 = jnp.concatenate([u_tab[0].reshape(-1, dsub, LANES), v_tab[0].reshape(-1, dsub, LANES)],
                          axis=1).astype(BF16)
    out = _apply(idx, h1, gates, ln2_g[0].reshape(dsub, LANES), ln2_b[0].reshape(dsub, LANES), uv3,
                 tt=32, group=4)
    return out.reshape(bsz, s, d)
```

```python
import functools
import math

import numpy as np
import jax
import jax.numpy as jnp
from jax import lax
from jax.experimental import pallas as pl
from jax.experimental.pallas import tpu as pltpu

F32 = jnp.float32
BF16 = jnp.bfloat16
I32 = jnp.int32

N_META = 16
CHUNK = 64
SUB_BLOCK = 16
EPS = 1e-5
DEPTH = 1
ALPHA = (2 * DEPTH) ** 0.25
A_HEADS = 8
A_HEAD_DIM = 128
A_KV_RANK = 256
IDX_HEADS = 16
IDX_DIM = 64
TOPK_MAX = 256
Q_BLOCK = 128
REL_BUCKETS = 32
REL_MAX_DIST = 128
G_HEADS = 4
G_VAL_DIM = 256
G_KEY_DIM = 128
G_GATE_RANK = 16
G_GATE_NORM = 16.0
P_HEADS = 8
P_NKEYS = 128
P_QDIM = 256
P_TOPK = 16

LANES = 128
SUBLANES = 8
VMEM_LIMIT = 56 * 1024 * 1024

C_AQ, C_IQ, C_GV, C_GO = 0, 1024, 2048, 3072
C_GQ, C_GK = 4096, 4608
C_CKV = 5120
C_IDX = 5376
C_GR = 5504
N_COLS = 5632

INT_MIN = -2 ** 31
NT_DIMS = (((1,), (1,)), ((), ()))
TN_DIMS = (((0,), (0,)), ((), ()))


def _cparams(*sem):
    return pltpu.CompilerParams(dimension_semantics=sem, vmem_limit_bytes=VMEM_LIMIT)


def _layer_norm(x, g, b):
    mu = jnp.mean(x, axis=-1, keepdims=True)
    xc = x - mu
    var = jnp.mean(xc * xc, axis=-1, keepdims=True)
    return xc * lax.rsqrt(var + EPS) * g + b


def _ln_proj_kernel(x_ref, g_ref, b_ref, w_ref, h_ref, p_ref, hb_ref):
    @pl.when(pl.program_id(1) == 0)
    def _():
        y = _layer_norm(x_ref[...], g_ref[...], b_ref[...])
        h_ref[...] = y
        hb_ref[...] = y.astype(BF16)

    p_ref[...] = jnp.dot(hb_ref[...], w_ref[...], preferred_element_type=F32)


def _ln_proj(x2, g, b, wcat, tm, tn):
    m, d = x2.shape
    n = wcat.shape[1]
    return pl.pallas_call(
        _ln_proj_kernel,
        out_shape=(jax.ShapeDtypeStruct((m, d), F32), jax.ShapeDtypeStruct((m, n), F32)),
        grid=(m // tm, n // tn),
        in_specs=[pl.BlockSpec((tm, d), lambda i, j: (i, 0)),
                  pl.BlockSpec((1, d), lambda i, j: (0, 0)),
                  pl.BlockSpec((1, d), lambda i, j: (0, 0)),
                  pl.BlockSpec((d, tn), lambda i, j: (0, j))],
        out_specs=(pl.BlockSpec((tm, d), lambda i, j: (i, 0)),
                   pl.BlockSpec((tm, tn), lambda i, j: (i, j))),
        scratch_shapes=[pltpu.VMEM((tm, d), BF16)],
        compiler_params=_cparams("parallel", "arbitrary"),
        name="ln_proj",
    )(x2, g, b, wcat)


N_PAT = 5


def _t5_bucket(rel):
    nb = REL_BUCKETS // 2
    max_exact = nb // 2
    base = jnp.where(rel > 0, nb, 0)
    n = jnp.abs(rel)
    nf = jnp.maximum(n, 1).astype(jnp.float32)
    large = max_exact + (jnp.log(nf / max_exact) / math.log(REL_MAX_DIST / max_exact)
                         * (nb - max_exact)).astype(jnp.int32)
    large = jnp.minimum(large, nb - 1)
    return base + jnp.where(n < max_exact, n, large)


def _bias_patterns():
    i = jnp.arange(Q_BLOCK, dtype=I32)[:, None]
    j = jnp.arange(LANES, dtype=I32)[None, :]
    rels = [j - i, j - i - LANES, j - i - 2 * LANES, j - N_META - i, j - N_META - i - LANES]
    return jnp.stack([_t5_bucket(r) for r in rels])


def _check_bucket_saturation(max_dist):
    nb = REL_BUCKETS // 2
    max_exact = nb // 2
    n = np.arange(LANES + 1, max_dist + 1, dtype=np.float64)
    large = max_exact + np.floor(np.log(n / max_exact) / math.log(REL_MAX_DIST / max_exact) * (nb - max_exact))
    assert np.all(large >= nb - 1 + 0.5), "relative-position buckets do not saturate within one key tile"


def _bias_tab_kernel(bk_ref, rb_ref, o_ref):
    h = pl.program_id(0)
    for p in range(N_PAT):
        bk = bk_ref[p]
        acc = jnp.zeros((Q_BLOCK, LANES), F32)
        for b in range(REL_BUCKETS):
            acc = jnp.where(bk == b, rb_ref[b, h], acc)
        o_ref[0, p] = acc


def _bias_tab(rel_bias):
    return pl.pallas_call(
        _bias_tab_kernel,
        out_shape=jax.ShapeDtypeStruct((A_HEADS, N_PAT, Q_BLOCK, LANES), F32),
        grid=(A_HEADS,),
        in_specs=[pl.BlockSpec((N_PAT, Q_BLOCK, LANES), lambda h: (0, 0, 0)),
                  pl.BlockSpec(memory_space=pltpu.SMEM)],
        out_specs=pl.BlockSpec((1, N_PAT, Q_BLOCK, LANES), lambda h: (h, 0, 0, 0)),
        compiler_params=_cparams("arbitrary"),
        name="bias_tab",
    )(_bias_patterns(), rel_bias)


def _dsa_kernel(aq_ref, iq_ref, idxq_ref, ckv_ref, idxk_ref, ckvm_ref, idxm_ref, wuk_ref, wuv_ref,
                bt_ref, o_ref,
                ik_scr, ckvb_scr, iqs_scr, l_scr, key_scr, el_scr, madd_scr, ql_scr, lg_scr, p_scr,
                den_scr, *, q_lo, width, topk):
    nt = 1 + width // LANES
    wt = nt * LANES
    qi = pl.program_id(1) + q_lo

    @pl.when(pl.program_id(1) == 0)
    def _stage_keys():
        ik_scr[0:LANES, :] = idxm_ref[:, 0:IDX_DIM].astype(BF16)
        ik_scr[LANES:wt, :] = idxk_ref[0, :, 0:IDX_DIM].astype(BF16)
        ckvb_scr[0:LANES, :] = ckvm_ref[...].astype(BF16)
        ckvb_scr[LANES:wt, :] = ckv_ref[0].astype(BF16)

    row = lax.broadcasted_iota(I32, (Q_BLOCK, LANES), 0)
    lane = lax.broadcasted_iota(I32, (Q_BLOCK, LANES), 1)
    kb = ((qi * Q_BLOCK + row) // CHUNK + 1) * CHUNK

    iq = iq_ref[0].astype(BF16)
    for h in range(IDX_HEADS):
        iqs_scr[h * Q_BLOCK:(h + 1) * Q_BLOCK, :] = iq[:, h * IDX_DIM:(h + 1) * IDX_DIM]
    iw = idxq_ref[0][:, IDX_DIM:IDX_DIM + IDX_HEADS] * (IDX_HEADS ** -0.5 * IDX_DIM ** -0.5)
    iw_b = [jnp.broadcast_to(iw[:, h:h + 1], (Q_BLOCK, LANES)) for h in range(IDX_HEADS)]
    chunks = [(c0, min(2 * LANES, wt - c0)) for c0 in range(0, wt, 2 * LANES)]
    for ci, (c0, cw) in enumerate(chunks):
        l_buf = l_scr.at[ci % 2]
        l_buf[:, 0:cw] = lax.dot_general(iqs_scr[...], ik_scr[c0:c0 + cw, :], NT_DIMS,
                                         preferred_element_type=F32)
        for kt in range(c0 // LANES, (c0 + cw) // LANES):
            cols = slice(kt * LANES - c0, (kt + 1) * LANES - c0)
            acc = jnp.zeros((Q_BLOCK, LANES), F32)
            for h in range(IDX_HEADS):
                acc = acc + jnp.maximum(l_buf[h * Q_BLOCK:(h + 1) * Q_BLOCK, cols], 0.0) * iw_b[h]
            adm = (lane < N_META) if kt == 0 else (lane + (kt - 1) * LANES < kb)
            bits = pltpu.bitcast(acc + 0.0, I32)
            key = jnp.where(bits < 0, bits ^ 0x7FFFFFFF, bits)
            key_scr[:, kt * LANES:(kt + 1) * LANES] = jnp.where(adm, key, INT_MIN)

    def count(pred):
        acc = jnp.zeros((Q_BLOCK, LANES), F32)
        for kt in range(nt):
            acc = acc + jnp.where(pred(kt), 1.0, 0.0)
        return jnp.sum(acc, axis=1, keepdims=True)

    def key_tile(kt):
        return key_scr[:, kt * LANES:(kt + 1) * LANES]

    def val_step(it, thr):
        cand = thr + jnp.left_shift(jnp.int32(1), 31 - it)
        cnt = count(lambda kt: key_tile(kt) >= cand)
        return jnp.where(cnt >= topk, cand, thr)

    thr = lax.fori_loop(0, 32, val_step, jnp.full((Q_BLOCK, 1), INT_MIN, I32))
    n_ge = count(lambda kt: key_tile(kt) >= thr)
    thr_adm = jnp.maximum(thr, INT_MIN + 1)
    for kt in range(nt):
        madd_scr[:, kt * LANES:(kt + 1) * LANES] = jnp.where(key_tile(kt) >= thr_adm, 0.0, -jnp.inf)
    tied = jnp.where((n_ge > topk) & (thr > INT_MIN), 1.0, 0.0)

    @pl.when(jnp.max(tied) > 0.0)
    def _break_ties():
        need = topk - count(lambda kt: key_tile(kt) > thr)
        big = jnp.int32(2 ** 30)
        for kt in range(nt):
            el_scr[:, kt * LANES:(kt + 1) * LANES] = jnp.where(key_tile(kt) == thr, lane + kt * LANES, big)

        def el_tile(kt):
            return el_scr[:, kt * LANES:(kt + 1) * LANES]

        nbits = max(1, int(wt - 1).bit_length())

        def pos_step(it, x):
            cand = x + jnp.left_shift(jnp.int32(1), nbits - 1 - it)
            cnt = count(lambda kt: el_tile(kt) < cand)
            return jnp.where(cnt < need, cand, x)

        xpos = lax.fori_loop(0, nbits, pos_step, jnp.zeros((Q_BLOCK, 1), I32))
        for kt in range(nt):
            sel = ((key_tile(kt) > thr) | (el_tile(kt) <= xpos)) & (key_tile(kt) > INT_MIN)
            madd_scr[:, kt * LANES:(kt + 1) * LANES] = jnp.where(sel, 0.0, -jnp.inf)

    aq = aq_ref[0].astype(BF16)
    for h in range(A_HEADS):
        qh = lax.dot_general(aq[:, h * A_HEAD_DIM:(h + 1) * A_HEAD_DIM], wuk_ref[h], NT_DIMS,
                             preferred_element_type=F32) * (A_HEAD_DIM ** -0.5)
        ql_scr[h * Q_BLOCK:(h + 1) * Q_BLOCK, :] = qh.astype(BF16)
    for c0, cw in chunks:
        lg_scr[:, c0:c0 + cw] = lax.dot_general(
            ql_scr[...], ckvb_scr[c0:c0 + cw, :], NT_DIMS, preferred_element_type=F32)

    def head_body(h, carry):
        r0 = pl.multiple_of(h * Q_BLOCK, Q_BLOCK)
        rows = pl.ds(r0, Q_BLOCK)
        m = jnp.full((Q_BLOCK, LANES), -jnp.inf, F32)
        for kt in range(nt):
            cols = slice(kt * LANES, (kt + 1) * LANES)
            pat = (3 + jnp.minimum(qi, 1)) if kt == 0 else jnp.clip(qi - (kt - 1), 0, 2)
            l = lg_scr[rows, cols] + bt_ref[h, pat] + madd_scr[:, cols]
            lg_scr[rows, cols] = l
            m = jnp.maximum(m, l)
        mrow = jnp.max(m, axis=1, keepdims=True)
        ssum = jnp.zeros((Q_BLOCK, LANES), F32)
        for kt in range(nt):
            cols = slice(kt * LANES, (kt + 1) * LANES)
            e = jnp.exp(lg_scr[rows, cols] - mrow)
            ssum = ssum + e
            p_scr[rows, cols] = e.astype(BF16)
        den_scr[rows, :] = jnp.broadcast_to(jnp.sum(ssum, axis=1, keepdims=True), (Q_BLOCK, LANES))
        return carry

    lax.fori_loop(0, A_HEADS, head_body, 0)
    o = jnp.dot(p_scr[...], ckvb_scr[...], preferred_element_type=F32)
    for h in range(A_HEADS):
        rs = slice(h * Q_BLOCK, (h + 1) * Q_BLOCK)
        oh = o[rs] / den_scr[rs, 0:1]
        o_ref[0, :, h * A_HEAD_DIM:(h + 1) * A_HEAD_DIM] = jnp.dot(
            oh.astype(BF16), wuv_ref[h], preferred_element_type=F32)


def _dsa(proj3, ckv_meta, idx_meta, wuk, wuv, btab, q_lo, n_q, width, topk):
    bsz, s, _ = proj3.shape
    nt = 1 + width // LANES
    wt = nt * LANES
    hq = A_HEADS * Q_BLOCK
    kern = functools.partial(_dsa_kernel, q_lo=q_lo, width=width, topk=topk)
    return pl.pallas_call(
        kern,
        out_shape=jax.ShapeDtypeStruct((bsz, n_q * Q_BLOCK, A_HEADS * A_HEAD_DIM), F32),
        grid=(bsz, n_q),
        in_specs=[
            pl.BlockSpec((1, Q_BLOCK, 1024), lambda b, q: (b, q + q_lo, C_AQ // 1024)),
            pl.BlockSpec((1, Q_BLOCK, 1024), lambda b, q: (b, q + q_lo, C_IQ // 1024)),
            pl.BlockSpec((1, Q_BLOCK, LANES), lambda b, q: (b, q + q_lo, C_IDX // LANES)),
            pl.BlockSpec((1, width, A_KV_RANK), lambda b, q: (b, 0, C_CKV // A_KV_RANK)),
            pl.BlockSpec((1, width, LANES), lambda b, q: (b, 0, C_IDX // LANES)),
            pl.BlockSpec((LANES, A_KV_RANK), lambda b, q: (0, 0)),
            pl.BlockSpec((LANES, LANES), lambda b, q: (0, 0)),
            pl.BlockSpec((A_HEADS, A_KV_RANK, A_HEAD_DIM), lambda b, q: (0, 0, 0)),
            pl.BlockSpec((A_HEADS, A_KV_RANK, A_HEAD_DIM), lambda b, q: (0, 0, 0)),
            pl.BlockSpec((A_HEADS, N_PAT, Q_BLOCK, LANES), lambda b, q: (0, 0, 0, 0)),
        ],
        out_specs=pl.BlockSpec((1, Q_BLOCK, A_HEADS * A_HEAD_DIM), lambda b, q: (b, q, 0)),
        scratch_shapes=[
            pltpu.VMEM((wt, IDX_DIM), BF16),
            pltpu.VMEM((wt, A_KV_RANK), BF16),
            pltpu.VMEM((IDX_HEADS * Q_BLOCK, IDX_DIM), BF16),
            pltpu.VMEM((2, IDX_HEADS * Q_BLOCK, 2 * LANES), F32),
            pltpu.VMEM((Q_BLOCK, wt), I32),
            pltpu.VMEM((Q_BLOCK, wt), I32),
            pltpu.VMEM((Q_BLOCK, wt), F32),
            pltpu.VMEM((hq, A_KV_RANK), BF16),
            pltpu.VMEM((hq, wt), F32),
            pltpu.VMEM((hq, wt), BF16),
            pltpu.VMEM((hq, LANES), F32),
        ],
        compiler_params=_cparams("parallel", "arbitrary"),
        name=f"dsa_w{width}",
    )(proj3, proj3, proj3, proj3, proj3, ckv_meta, idx_meta, wuk, wuv, btab)


def _gla_kernel(gq_ref, gk_ref, gv_ref, go_ref, gr_ref, gkm_ref, gvm_ref, grm_ref, w2_ref, bg_ref,
                ng_ref, o_ref, st_scr):
    w2 = w2_ref[...].astype(BF16)

    def log_decay(gr):
        z = jnp.dot(gr.astype(BF16), w2, preferred_element_type=F32) + bg_ref[...]
        return (jnp.minimum(z, 0.0) - jnp.log(1.0 + jnp.exp(-jnp.abs(z)))) * (1.0 / G_GATE_NORM)

    def cumsum_rows(g):
        n = g.shape[0]
        tri = (lax.broadcasted_iota(I32, (n, n), 0) >= lax.broadcasted_iota(I32, (n, n), 1)).astype(F32)
        return jnp.dot(tri, g, precision=lax.Precision.HIGHEST, preferred_element_type=F32)

    @pl.when(pl.program_id(1) == 0)
    def _init_state():
        bm = cumsum_rows(log_decay(grm_ref[:, 0:G_GATE_RANK]))
        kd = gkm_ref[...] * jnp.exp(bm[N_META - 1:N_META, :] - bm)
        for h in range(G_HEADS):
            st_scr[h] = lax.dot_general(
                gvm_ref[:, h * G_VAL_DIM:(h + 1) * G_VAL_DIM].astype(BF16),
                kd[:, h * G_KEY_DIM:(h + 1) * G_KEY_DIM].astype(BF16), TN_DIMS,
                preferred_element_type=F32)

    b = cumsum_rows(log_decay(gr_ref[0][:, 0:G_GATE_RANK]))
    row = lax.broadcasted_iota(I32, (CHUNK, CHUNK), 0)
    col = lax.broadcasted_iota(I32, (CHUNK, CHUNK), 1)
    col_sb = lax.broadcasted_iota(I32, (SUB_BLOCK, CHUNK), 1)
    for h in range(G_HEADS):
        ks = slice(h * G_KEY_DIM, (h + 1) * G_KEY_DIM)
        vs = slice(h * G_VAL_DIM, (h + 1) * G_VAL_DIM)
        q = gq_ref[0][:, ks] * (G_KEY_DIM ** -0.5)
        k = gk_ref[0][:, ks]
        v = gv_ref[0][:, vs].astype(BF16)
        bh = b[:, ks]
        st = st_scr[h]
        o = lax.dot_general((q * jnp.exp(bh)).astype(BF16), st.astype(BF16), NT_DIMS,
                            preferred_element_type=F32)
        blocks = []
        for ib in range(CHUNK // SUB_BLOCK):
            rs = slice(ib * SUB_BLOCK, (ib + 1) * SUB_BLOCK)
            qi, bi = q[rs], bh[rs]
            ai = jnp.zeros((SUB_BLOCK, CHUNK), F32)
            if ib > 0:
                br = bi[0:1, :]
                kt = k * jnp.exp(jnp.minimum(br - bh, 0.0))
                ai = lax.dot_general(qi * jnp.exp(bi - br), kt, NT_DIMS, precision=lax.Precision.HIGHEST,
                                     preferred_element_type=F32)
                ai = jnp.where(col_sb < ib * SUB_BLOCK, ai, 0.0)
            for j in range(ib * SUB_BLOCK, (ib + 1) * SUB_BLOCK):
                w = qi * k[j:j + 1, :] * jnp.exp(jnp.minimum(bi - bh[j:j + 1, :], 0.0))
                ai = jnp.where(col_sb == j, jnp.sum(w, axis=1, keepdims=True), ai)
            blocks.append(ai)
        a = jnp.concatenate(blocks, axis=0)
        a = jnp.where(row >= col, a, 0.0)
        o = o + jnp.dot(a.astype(BF16), v, preferred_element_type=F32)
        bl = bh[CHUNK - 1:CHUNK, :]
        kd = k * jnp.exp(bl - bh)
        st_scr[h] = st * jnp.exp(bl) + lax.dot_general(v, kd.astype(BF16), TN_DIMS,
                                                       preferred_element_type=F32)
        on = o * lax.rsqrt(jnp.mean(o * o, axis=1, keepdims=True) + EPS) * ng_ref[...]
        og = go_ref[0][:, vs]
        o_ref[0, :, vs] = on * (og * (1.0 / (1.0 + jnp.exp(-og))))


def _gla(proj3, proj_meta, w_gk2, b_gk, norm_g):
    bsz, s, _ = proj3.shape
    hk = G_HEADS * G_KEY_DIM
    hv = G_HEADS * G_VAL_DIM
    gkm = proj_meta[:, C_GK:C_GK + hk]
    gvm = proj_meta[:, C_GV:C_GV + hv]
    grm = proj_meta[:, C_GR:C_GR + LANES]
    full = lambda shp: pl.BlockSpec(shp, lambda b, c: (0,) * len(shp))
    return pl.pallas_call(
        _gla_kernel,
        out_shape=jax.ShapeDtypeStruct((bsz, s, hv), F32),
        grid=(bsz, s // CHUNK),
        in_specs=[
            pl.BlockSpec((1, CHUNK, hk), lambda b, c: (b, c, C_GQ // hk)),
            pl.BlockSpec((1, CHUNK, hk), lambda b, c: (b, c, C_GK // hk)),
            pl.BlockSpec((1, CHUNK, hv), lambda b, c: (b, c, C_GV // hv)),
            pl.BlockSpec((1, CHUNK, hv), lambda b, c: (b, c, C_GO // hv)),
            pl.BlockSpec((1, CHUNK, LANES), lambda b, c: (b, c, C_GR // LANES)),
            full((N_META, hk)), full((N_META, hv)), full((N_META, LANES)),
            full((G_GATE_RANK, hk)), full((1, hk)), full((1, G_VAL_DIM)),
        ],
        out_specs=pl.BlockSpec((1, CHUNK, hv), lambda b, c: (b, c, 0)),
        scratch_shapes=[pltpu.VMEM((G_HEADS, G_VAL_DIM, G_KEY_DIM), F32)],
        compiler_params=_cparams("parallel", "arbitrary"),
        name="gla",
    )(proj3, proj3, proj3, proj3, proj3, gkm, gvm, grm, w_gk2, b_gk, norm_g)


def _out_ln_kernel(ya_ref, yb_ref, h_ref, wa_ref, wb_ref, g_ref, b_ref, o_ref):
    mix = (jnp.dot(ya_ref[...].astype(BF16), wa_ref[...], preferred_element_type=F32)
           + jnp.dot(yb_ref[...].astype(BF16), wb_ref[...], preferred_element_type=F32))
    o_ref[...] = _layer_norm(ALPHA * h_ref[...] + mix, g_ref[...], b_ref[...])


def _out_ln(ya, yb, h, wa, wb, g, b, tm):
    m, d = h.shape
    ka, kb = ya.shape[1], yb.shape[1]
    return pl.pallas_call(
        _out_ln_kernel,
        out_shape=jax.ShapeDtypeStruct((m, d), F32),
        grid=(m // tm,),
        in_specs=[pl.BlockSpec((tm, ka), lambda i: (i, 0)),
                  pl.BlockSpec((tm, kb), lambda i: (i, 0)),
                  pl.BlockSpec((tm, d), lambda i: (i, 0)),
                  pl.BlockSpec((ka, d), lambda i: (0, 0)),
                  pl.BlockSpec((kb, d), lambda i: (0, 0)),
                  pl.BlockSpec((1, d), lambda i: (0, 0)),
                  pl.BlockSpec((1, d), lambda i: (0, 0))],
        out_specs=pl.BlockSpec((tm, d), lambda i: (i, 0)),
        compiler_params=_cparams("parallel"),
        name="out_ln",
    )(ya, yb, h, wa, wb, g, b)


RANK_NONE = 2 ** 20


def _top_rows(s, rank, k):
    slot = lax.broadcasted_iota(I32, (k, LANES), 0)
    vals = jnp.zeros((k, LANES), F32)
    idxs = jnp.zeros((k, LANES), I32)
    for r in range(k):
        m = jnp.max(s, axis=0, keepdims=True)
        im = jnp.min(jnp.where(s == m, rank, RANK_NONE), axis=0, keepdims=True)
        vals = jnp.where(slot == r, m, vals)
        idxs = jnp.where(slot == r, im, idxs)
        s = jnp.where(rank == im, -jnp.inf, s)
    return vals, idxs


def _product_candidates(s1, i1, s2, i2):
    sub = lax.broadcasted_iota(I32, (SUBLANES, LANES), 0)
    lo = sub < 4
    s2a, s2b = s2[0:SUBLANES], s2[SUBLANES:2 * SUBLANES]
    i2a, i2b = i2[0:SUBLANES], i2[SUBLANES:2 * SUBLANES]
    s2d = jnp.where(lo, s2a, pltpu.roll(s2a, 4, 0))
    i2d = jnp.where(lo, i2a, pltpu.roll(i2a, 4, 0))
    row = lambda v, a: jnp.broadcast_to(v[a:a + 1, :], (SUBLANES, LANES))
    pair = lambda v, a: jnp.where(lo, row(v, a), row(v, a + 1))
    e1 = i1 * P_NKEYS
    groups = [
        (row(s1, 0) + s2a, row(e1, 0) + i2a, sub, None),
        (row(s1, 0) + s2b, row(e1, 0) + i2b, sub + SUBLANES, None),
        (row(s1, 1) + s2a, row(e1, 1) + i2a, sub + P_TOPK, None),
        (row(s1, 2) + s2a, row(e1, 2) + i2a, sub + 2 * P_TOPK, sub < 5),
        (row(s1, 3) + s2a, row(e1, 3) + i2a, sub + 3 * P_TOPK, sub < 4),
        (pair(s1, 4) + s2d, pair(e1, 4) + i2d, jnp.where(lo, sub + 4 * P_TOPK, sub - 4 + 5 * P_TOPK),
         (sub < 3) | ((sub >= 4) & (sub < 6))),
        (pair(s1, 6) + s2d, pair(e1, 6) + i2d, jnp.where(lo, sub + 6 * P_TOPK, sub - 4 + 7 * P_TOPK),
         (sub < 2) | ((sub >= 4) & (sub < 6))),
        (s1[SUBLANES:] + row(s2, 0), e1[SUBLANES:] + row(i2, 0), (sub + SUBLANES) * P_TOPK, None),
    ]
    vals, eids, ranks = [], [], []
    for v, e, rk, ok in groups:
        vals.append(v if ok is None else jnp.where(ok, v, -jnp.inf))
        ranks.append(rk if ok is None else jnp.where(ok, rk, RANK_NONE))
        eids.append(e)
    cat = lambda xs: jnp.concatenate(xs, axis=0)
    return cat(vals), cat(ranks), cat(eids)


def _route_kernel(h_ref, wq_ref, sk_ref, e_ref, g_ref, qt_scr, *, n_half):
    qt = lax.dot_general(wq_ref[...], h_ref[...].astype(BF16), NT_DIMS, preferred_element_type=F32)
    for half in range(n_half):
        qt_scr[half] = qt[:, half * LANES:(half + 1) * LANES].astype(BF16)
    key_id = lax.broadcasted_iota(I32, (P_NKEYS, LANES), 0)
    slot = lax.broadcasted_iota(I32, (P_TOPK, LANES), 0)
    half_dim = P_QDIM // 2

    def one_head(h, carry):
        for half in range(n_half):
            unit(h, half)
        return carry

    def unit(h, half):
        tops = []
        for c in range(2):
            d0 = pl.multiple_of((h * 2 + c) * half_dim, half_dim)
            s = jnp.dot(sk_ref[h, c], qt_scr[half, pl.ds(d0, half_dim), :], preferred_element_type=F32)
            tops.append(_top_rows(s, key_id, P_TOPK))
        (s1, i1), (s2, i2) = tops
        cand, crank, cidx = _product_candidates(s1, i1, s2, i2)
        top, pos = _top_rows(cand, crank, P_TOPK)
        eidx = jnp.zeros((P_TOPK, LANES), I32)
        for r in range(P_TOPK):
            er = jnp.max(jnp.where(crank == pos[r:r + 1, :], cidx, -1), axis=0, keepdims=True)
            eidx = jnp.where(slot == r, er, eidx)
        ex = jnp.exp(top - top[0:1, :])
        gate = ex / jnp.sum(ex, axis=0, keepdims=True)
        r0 = pl.multiple_of(h * P_TOPK, P_TOPK)
        e_ref[half, pl.ds(r0, P_TOPK), :] = eidx
        g_ref[half, pl.ds(r0, P_TOPK), :] = gate

    lax.fori_loop(0, P_HEADS, one_head, 0)


def _route(h1, wq_t, sk, tm):
    m, d = h1.shape
    n_half = tm // LANES
    nk = P_HEADS * P_TOPK
    kern = functools.partial(_route_kernel, n_half=n_half)
    return pl.pallas_call(
        kern,
        out_shape=(jax.ShapeDtypeStruct((m // LANES, nk, LANES), I32),
                   jax.ShapeDtypeStruct((m // LANES, nk, LANES), F32)),
        grid=(m // tm,),
        in_specs=[pl.BlockSpec((tm, d), lambda i: (i, 0)),
                  pl.BlockSpec(wq_t.shape, lambda i: (0, 0)),
                  pl.BlockSpec(sk.shape, lambda i: (0, 0, 0, 0))],
        out_specs=(pl.BlockSpec((n_half, nk, LANES), lambda i: (i, 0, 0)),
                   pl.BlockSpec((n_half, nk, LANES), lambda i: (i, 0, 0))),
        scratch_shapes=[pltpu.VMEM((n_half, wq_t.shape[0], LANES), BF16)],
        compiler_params=_cparams("parallel"),
        name="route",
    )(h1, wq_t, sk)


def _gelu_tanh(x):
    return 0.5 * x * (1.0 + jnp.tanh(math.sqrt(2.0 / math.pi) * (x + 0.044715 * (x * x * x))))


def _apply_kernel(idx_ref, idxn_ref, x_ref, gate_ref, g_ref, b_ref, uv_hbm, o_ref, *scratch,
                  tt, group, nk, dsub):
    buf = scratch[:tt]
    nbuf = 2 * group
    zs, cb, ys, xd, xn = (scratch[tt + s * nbuf:tt + (s + 1) * nbuf] for s in range(5))
    sem = scratch[tt + 5 * nbuf]

    def load_x(j, scr):
        for c in range(dsub):
            scr[c:c + 1, :] = x_ref[j:j + 1, c * LANES:(c + 1) * LANES]
        return scr[...]

    i = pl.program_id(0)
    n = pl.num_programs(0)
    lane = lax.broadcasted_iota(I32, (SUBLANES, nk), 1)
    cnt = float(dsub * LANES)

    def issue(src_idx, t, slot):
        for k in range(nk):
            pltpu.make_async_copy(uv_hbm.at[src_idx[t, k]], buf[slot].at[k],
                                  sem.at[slot]).start(priority=k % 2)

    def wait(slot):
        pltpu.make_async_copy(uv_hbm.at[pl.ds(0, nk)], buf[slot], sem.at[slot]).wait()

    def dots(j):
        x = load_x(j, xd[j % nbuf])
        parts = [jnp.zeros((SUBLANES, nk), F32) for _ in range(4)]
        for k in range(nk):
            p = buf[j][k, 0:dsub, :].astype(F32) * x
            r = p[0:SUBLANES]
            for c in range(1, dsub // SUBLANES):
                r = r + p[c * SUBLANES:(c + 1) * SUBLANES]
            parts[k % 4] = jnp.where(lane == k, jnp.sum(r, axis=1, keepdims=True), parts[k % 4])
        zs[j % nbuf][...] = (parts[0] + parts[1]) + (parts[2] + parts[3])

    def coefs(j):
        act = jnp.sum(zs[j % nbuf][...], axis=0, keepdims=True)
        coef = _gelu_tanh(act) * gate_ref[j:j + 1, :]
        cb[j % nbuf][...] = jnp.transpose(jnp.broadcast_to(coef, (nk, nk)))

    def mix(j):
        accs = [None] * 8
        for k in range(nk):
            term = buf[j][k, dsub:2 * dsub, :].astype(F32) * cb[j % nbuf][k:k + 1, :]
            accs[k % 8] = term if accs[k % 8] is None else accs[k % 8] + term
        ys[j % nbuf][...] = ((accs[0] + accs[1]) + (accs[2] + accs[3])) + ((accs[4] + accs[5]) + (accs[6] + accs[7]))

    def norm(j):
        r = ALPHA * load_x(j, xn[j % nbuf]) + ys[j % nbuf][...]
        mu = jnp.sum(jnp.sum(r, axis=1, keepdims=True), axis=0, keepdims=True) / cnt
        rc = r - mu
        var = jnp.sum(jnp.sum(rc * rc, axis=1, keepdims=True), axis=0, keepdims=True) / cnt
        y = rc * lax.rsqrt(var + EPS) * g_ref[...] + b_ref[...]
        for c in range(dsub):
            o_ref[j:j + 1, c * LANES:(c + 1) * LANES] = y[c:c + 1, :]

    depth = 3
    ng = tt // group
    toks = lambda g: range(g * group, (g + 1) * group)

    @pl.when(i == 0)
    def _prologue():
        for j in range(tt - depth * group):
            issue(idx_ref, j, j)

    def body(last):
        for m in range(ng + depth):
            if m < ng:
                for j in toks(m):
                    wait(j)
                for j in toks(m):
                    if m < depth:
                        issue(idx_ref, j + tt - depth * group, j + tt - depth * group)
                    elif not last:
                        issue(idxn_ref, j - depth * group, j - depth * group)
                for j in toks(m):
                    dots(j)
            for stage, fn in ((1, coefs), (2, mix), (3, norm)):
                if 0 <= m - stage < ng:
                    for j in toks(m - stage):
                        fn(j)

    @pl.when(i + 1 < n)
    def _steady():
        body(False)

    @pl.when(i + 1 == n)
    def _final():
        body(True)


def _apply(idx, h1, gates, g3, b3, uv3, tt, group):
    m, nk = idx.shape
    d = h1.shape[1]
    dsub = d // LANES
    n_steps = m // tt
    nbuf = 2 * group
    kern = functools.partial(_apply_kernel, tt=tt, group=group, nk=nk, dsub=dsub)
    return pl.pallas_call(
        kern,
        out_shape=jax.ShapeDtypeStruct(h1.shape, F32),
        grid=(n_steps,),
        in_specs=[pl.BlockSpec((tt, nk), lambda i: (i, 0), memory_space=pltpu.SMEM),
                  pl.BlockSpec((tt, nk), lambda i: (jnp.minimum(i + 1, n_steps - 1), 0),
                               memory_space=pltpu.SMEM),
                  pl.BlockSpec((tt, d), lambda i: (i, 0)),
                  pl.BlockSpec((tt, nk), lambda i: (i, 0)),
                  pl.BlockSpec((dsub, LANES), lambda i: (0, 0)),
                  pl.BlockSpec((dsub, LANES), lambda i: (0, 0)),
                  pl.BlockSpec(memory_space=pl.ANY)],
        out_specs=pl.BlockSpec((tt, d), lambda i: (i, 0)),
        scratch_shapes=[pltpu.VMEM((nk, 2 * dsub, LANES), BF16) for _ in range(tt)]
        + [pltpu.VMEM((SUBLANES, nk), F32)] * nbuf + [pltpu.VMEM((nk, LANES), F32)] * nbuf
        + [pltpu.VMEM((dsub, LANES), F32)] * (3 * nbuf) + [pltpu.SemaphoreType.DMA((tt,))],
        compiler_params=_cparams("arbitrary"),
        name="peer_apply",
    )(idx, idx, h1, gates, g3, b3, uv3)


def _regroup_w_in(w):
    d = w.shape[0]
    widths = (A_HEADS * A_HEAD_DIM, A_KV_RANK, IDX_HEADS * IDX_DIM, IDX_DIM, IDX_HEADS,
              G_HEADS * G_KEY_DIM, G_HEADS * G_KEY_DIM, G_HEADS * G_VAL_DIM, G_GATE_RANK,
              G_HEADS * G_VAL_DIM)
    offs = np.concatenate([[0], np.cumsum(widths)])
    a_q, a_ckv, i_q, i_k, i_w, g_q, g_k, g_v, g_r, g_o = [w[:, offs[n]:offs[n + 1]] for n in range(10)]
    z = lambda n: jnp.zeros((d, n), w.dtype)
    cat = jnp.concatenate([a_q, i_q, g_v, g_o, g_q, g_k, a_ckv,
                           i_k, i_w, z(LANES - IDX_DIM - IDX_HEADS),
                           g_r, z(LANES - G_GATE_RANK)], axis=1)
    assert cat.shape[1] == N_COLS
    return cat.astype(BF16)


def _dsa_groups(n_qblocks, n_groups):
    per = -(-n_qblocks // n_groups)
    return [(lo, min(per, n_qblocks - lo)) for lo in range(0, n_qblocks, per)]


def kernel(x, meta_tokens, ln0_g, ln0_b, rel_bias, w_in, w_uk, w_uv, w_gk2, b_gk, gla_norm_g, w_out,
           ln1_g, ln1_b, w_pq, sub_keys, u_tab, v_tab, ln2_g, ln2_b):
    bsz, s, d = x.shape
    t = bsz * s
    topk = min(TOPK_MAX, s // 4)
    assert s % Q_BLOCK == 0 and d % (SUBLANES * LANES) == 0
    _check_bucket_saturation(s + N_META)
    row2 = lambda v: v.reshape(1, -1)

    wcat = _regroup_w_in(w_in[0])
    h, proj = _ln_proj(x.reshape(t, d), row2(ln0_g), row2(ln0_b), wcat, tm=1024, tn=512)
    _, proj_meta = _ln_proj(meta_tokens, row2(ln0_g), row2(ln0_b), wcat, tm=N_META, tn=512)
    proj3 = proj.reshape(bsz, s, N_COLS)

    pad_rows = lambda a: jnp.pad(a, ((0, LANES - N_META), (0, 0)))
    ckv_meta = pad_rows(proj_meta[:, C_CKV:C_CKV + A_KV_RANK])
    idx_meta = pad_rows(proj_meta[:, C_IDX:C_IDX + LANES])
    btab = _bias_tab(rel_bias)
    wuk = w_uk[0].astype(BF16)
    wuv = w_uv[0].astype(BF16)
    ya_parts = []
    for q_lo, n_q in _dsa_groups(s // Q_BLOCK, 8):
        width = (q_lo + n_q) * Q_BLOCK
        ya_parts.append(_dsa(proj3, ckv_meta, idx_meta, wuk, wuv, btab, q_lo, n_q, width, topk))
    y_a = jnp.concatenate(ya_parts, axis=1).reshape(t, A_HEADS * A_HEAD_DIM)

    y_b = _gla(proj3, proj_meta, w_gk2[0], row2(b_gk[0]), row2(gla_norm_g[0])).reshape(t, -1)

    wo = w_out[0].astype(BF16)
    ka = A_HEADS * A_HEAD_DIM
    h1 = _out_ln(y_a, y_b, h, wo[:ka], wo[ka:], row2(ln1_g[0]), row2(ln1_b[0]), tm=256)

    wq_t = w_pq[0].T.astype(BF16)
    eidx, gates = _route(h1, wq_t, sub_keys[0].astype(BF16), tm=512)
    nk = P_HEADS * P_TOPK
    idx = eidx.transpose(0, 2, 1).reshape(t, nk)
    gates = gates.transpose(0, 2, 1).reshape(t, nk)
    dsub = d // LANES
    uv3 = jnp.concatenate([u_tab[0].reshape(-1, dsub, LANES), v_tab[0].reshape(-1, dsub, LANES)],
                          axis=1).astype(BF16)
    out = _apply(idx, h1, gates, ln2_g[0].reshape(dsub, LANES), ln2_b[0].reshape(dsub, LANES), uv3,
                 tt=32, group=4)
    return out.reshape(bsz, s, d)
```
